```python
import math
import jax, jax.numpy as jnp
from jax import lax
import numpy as np

D_MODEL = 1024
BATCH = 8
SEQ = 2048
DEPTH = 4
DEC_BATCH = 32
DEC_SEQ = 2048
PAST_LEN = 128

N_META = 16
GRID_W = 64
ROPE_THETA = 10000.0
RMS_EPS = 1e-6
DA_HEADS = 4
DA_HEAD = 64
DA_QK = DA_HEADS * 2 * DA_HEAD
DA_V = DA_HEADS * 2 * DA_HEAD
DA_BLOCK = 128
NA_HEADS = 8
NA_HEAD = 64
NA_W = NA_HEADS * NA_HEAD
NA_KH_MAX = 8
NA_KW = 16
SPLIT_SIZES = (DA_QK, DA_QK, DA_V, NA_W, NA_W, NA_W, D_MODEL, D_MODEL)
IN_COLS = sum(SPLIT_SIZES)
SPLITS = tuple(int(s) for s in np.cumsum(SPLIT_SIZES)[:-1])
PEER_HEADS = 8
PEER_NKEYS = 128
PEER_EXPERTS = PEER_NKEYS * PEER_NKEYS
PEER_TOPK = 16
PEER_DKEY = 256
PEER_HALF = PEER_DKEY // 2
PEER_BLOCK = 128

kernel_name = "hybrid_diffattn_natten_peer_encoder"

F32 = jnp.float32


def _rmsnorm(x, g):
    xf = x.astype(F32)
    y = xf * lax.rsqrt(jnp.mean(xf * xf, axis=-1, keepdims=True) + RMS_EPS)
    return (y * g.astype(F32)).astype(x.dtype)


def _rope_tables(L):
    half = DA_HEAD // 2
    inv = 1.0 / (ROPE_THETA ** (jnp.arange(half, dtype=F32) * 2.0 / DA_HEAD))
    ang = jnp.arange(L, dtype=F32)[:, None] * inv[None, :]
    return (jnp.cos(ang).reshape(1, L, 1, 1, half), jnp.sin(ang).reshape(1, L, 1, 1, half))


def _rope(x, cos, sin):
    half = DA_HEAD // 2
    xf = x.astype(F32)
    x1, x2 = xf[..., :half], xf[..., half:]
    return jnp.concatenate([x1 * cos - x2 * sin, x2 * cos + x1 * sin], axis=-1)


def _diff_attention(q, k, v, lam, subln_g, lam_init):
    B, L = q.shape[0], q.shape[1]
    S = L - N_META
    scale = DA_HEAD ** -0.5
    k32 = k.astype(F32)
    v32 = v.astype(F32)

    def attend(qb):
        s = jnp.einsum('bqhmd,bkhmd->bhmqk', qb.astype(F32), k32) * scale
        p = jax.nn.softmax(s, axis=-1)
        a = p[:, :, 0] - lam * p[:, :, 1]
        return jnp.einsum('bhqk,bkhe->bqhe', a, v32)

    o_meta = attend(q[:, :N_META])
    qr = q[:, N_META:].reshape(B, S // DA_BLOCK, DA_BLOCK, DA_HEADS, 2, DA_HEAD)
    o_real = lax.map(attend, jnp.moveaxis(qr, 1, 0))
    o_real = jnp.moveaxis(o_real, 0, 1).reshape(B, S, DA_HEADS, 2 * DA_HEAD)
    o = jnp.concatenate([o_meta, o_real], axis=1)
    o = _rmsnorm(o, subln_g) * (1.0 - lam_init)
    return o.reshape(B, L, DA_V)


def _neighbourhood_attention(q, k, v, rpb):
    B, L = q.shape[0], q.shape[1]
    S = L - N_META
    ROWS = S // GRID_W
    KH = min(NA_KH_MAX, ROWS)
    scale = NA_HEAD ** -0.5
    q32, k32, v32 = q.astype(F32), k.astype(F32), v.astype(F32)
    qm, km, vm = q32[:, :N_META], k32[:, :N_META], v32[:, :N_META]
    qg = q32[:, N_META:].reshape(B, ROWS, GRID_W, NA_HEADS, NA_HEAD)
    kg = k32[:, N_META:].reshape(B, ROWS, GRID_W, NA_HEADS, NA_HEAD)
    vg = v32[:, N_META:].reshape(B, ROWS, GRID_W, NA_HEADS, NA_HEAD)

    r = np.arange(ROWS)
    row_start = np.clip(r - KH // 2, 0, ROWS - KH)
    row_off = row_start[:, None] + np.arange(KH)[None, :] - r[:, None] + (NA_KH_MAX - 1)
    c = np.arange(GRID_W)
    col_start = np.clip(c - NA_KW // 2, 0, GRID_W - NA_KW)
    j = np.arange(GRID_W)
    col_valid = (j[None, :] >= col_start[:, None]) & (j[None, :] < col_start[:, None] + NA_KW)
    col_off = np.clip(j[None, :] - c[:, None], -(NA_KW - 1), NA_KW - 1) + (NA_KW - 1)
    col_bias = rpb.astype(F32)[:, :, col_off]
    col_bias = jnp.where(jnp.asarray(col_valid)[None, None], col_bias, -jnp.inf)

    def row_fn(args):
        q_row, rs, roff = args
        k_blk = lax.dynamic_slice_in_dim(kg, rs, KH, axis=1)
        v_blk = lax.dynamic_slice_in_dim(vg, rs, KH, axis=1)
        s = jnp.einsum('bqhd,bikhd->bhqik', q_row, k_blk) * scale
        s = s + jnp.transpose(col_bias[:, roff], (0, 2, 1, 3))[None]
        sm = jnp.einsum('bqhd,bmhd->bhqm', q_row, km) * scale
        p = jax.nn.softmax(jnp.concatenate([s.reshape(B, NA_HEADS, GRID_W, KH * GRID_W), sm], axis=-1), axis=-1)
        pg = p[..., :KH * GRID_W].reshape(B, NA_HEADS, GRID_W, KH, GRID_W)
        pm = p[..., KH * GRID_W:]
        return (jnp.einsum('bhqik,bikhd->bqhd', pg, v_blk)
                + jnp.einsum('bhqm,bmhd->bqhd', pm, vm))

    og = lax.map(row_fn, (jnp.transpose(qg, (1, 0, 2, 3, 4)),
                          jnp.asarray(row_start, dtype=jnp.int32),
                          jnp.asarray(row_off, dtype=jnp.int32)))
    og = jnp.transpose(og, (1, 0, 2, 3, 4)).reshape(B, S, NA_HEADS, NA_HEAD)
    pmm = jax.nn.softmax(jnp.einsum('bqhd,bmhd->bhqm', qm, km) * scale, axis=-1)
    om = jnp.einsum('bhqm,bmhd->bqhd', pmm, vm)
    return jnp.concatenate([om, og], axis=1).reshape(B, L, NA_W)


def _peer(xn, wq, keys, u, v):
    B, L, D = xn.shape
    xf = xn.reshape(B * L, D)
    T = xf.shape[0]
    pad = (-T) % PEER_BLOCK
    blocks = jnp.pad(xf, ((0, pad), (0, 0))).reshape(-1, PEER_BLOCK, D)
    keys32 = keys.astype(F32)

    def blk(xb):
        n = xb.shape[0]
        q = (xb @ wq).astype(F32).reshape(n, PEER_HEADS, 2, PEER_HALF)
        s = jnp.einsum('nhpd,hpkd->nhpk', q, keys32)
        s1, i1 = lax.top_k(s[:, :, 0], PEER_TOPK)
        s2, i2 = lax.top_k(s[:, :, 1], PEER_TOPK)
        cs = (s1[..., :, None] + s2[..., None, :]).reshape(n, PEER_HEADS, PEER_TOPK * PEER_TOPK)
        ci = (i1[..., :, None] * PEER_NKEYS + i2[..., None, :]).reshape(n, PEER_HEADS, PEER_TOPK * PEER_TOPK)
        sc, pos = lax.top_k(cs, PEER_TOPK)
        e = jnp.take_along_axis(ci, pos, axis=-1)
        g = jax.nn.softmax(sc, axis=-1)
        ue = jnp.take(u, e, axis=0)
        ve = jnp.take(v, e, axis=0)
        hid = jax.nn.gelu(jnp.einsum('nd,nhed->nhe', xb, ue).astype(F32), approximate=False)
        return jnp.einsum('nhe,nhed->nd', (g * hid).astype(ve.dtype), ve)

    out = lax.map(blk, blocks).reshape(-1, D)[:T]
    return out.reshape(B, L, D)


def _trunk(x, meta_tokens, norm_mix, w_in, lambda_q1, lambda_k1, lambda_q2, lambda_k2,
           subln_gain, na_rpb, w_branch_a, w_branch_b, w_out, norm_ffn,
           peer_wq, peer_keys, peer_u, peer_v, norm_final):
    dt = x.dtype
    B = x.shape[0]
    h = jnp.concatenate([jnp.broadcast_to(meta_tokens.astype(dt)[None], (B, N_META, D_MODEL)), x], axis=1)
    L = h.shape[1]
    cos, sin = _rope_tables(L)
    for l in range(DEPTH):
        a = _rmsnorm(h, norm_mix[l])
        z = a @ w_in[l]
        q_da, k_da, v_da, q_na, k_na, v_na, g_a, g_b = jnp.split(z, SPLITS, axis=-1)
        lam_init = 0.8 - 0.6 * math.exp(-0.3 * l)
        lam = (jnp.exp(jnp.sum(lambda_q1[l].astype(F32) * lambda_k1[l].astype(F32)))
               - jnp.exp(jnp.sum(lambda_q2[l].astype(F32) * lambda_k2[l].astype(F32))) + lam_init)
        q_da = _rope(q_da.reshape(B, L, DA_HEADS, 2, DA_HEAD), cos, sin)
        k_da = _rope(k_da.reshape(B, L, DA_HEADS, 2, DA_HEAD), cos, sin)
        v_da = v_da.reshape(B, L, DA_HEADS, 2 * DA_HEAD)
        o_da = _diff_attention(q_da, k_da, v_da, lam, subln_gain[l], lam_init).astype(dt)
        o_na = _neighbourhood_attention(q_na.reshape(B, L, NA_HEADS, NA_HEAD),
                                        k_na.reshape(B, L, NA_HEADS, NA_HEAD),
                                        v_na.reshape(B, L, NA_HEADS, NA_HEAD),
                                        na_rpb[l]).astype(dt)
        merged = jax.nn.sigmoid(g_a) * (o_da @ w_branch_a[l]) + jax.nn.sigmoid(g_b) * (o_na @ w_branch_b[l])
        h = h + (merged @ w_out[l]).astype(dt)
        c = _rmsnorm(h, norm_ffn[l])
        h = h + _peer(c, peer_wq[l], peer_keys[l], peer_u[l], peer_v[l]).astype(dt)
    return _rmsnorm(h, norm_final)[:, N_META:]


def setup_inputs(seed: int = 0) -> dict:
    key = jax.random.key(seed)
    ks = jax.random.split(key, 20)
    nrm = lambda k, shape, s: jax.random.normal(k, shape, dtype=F32) * s
    return {
        "x_prompt": nrm(ks[0], (BATCH, SEQ, D_MODEL), 1.0),
        "x_sample": nrm(ks[1], (DEC_BATCH, DEC_SEQ, D_MODEL), 1.0),
        "meta_tokens": nrm(ks[2], (N_META, D_MODEL), 1.0),
        "norm_mix": 1.0 + nrm(ks[3], (DEPTH, D_MODEL), 0.02),
        "w_in": nrm(ks[4], (DEPTH, D_MODEL, IN_COLS), D_MODEL ** -0.5),
        "lambda_q1": nrm(ks[5], (DEPTH, DA_HEAD), 0.1),
        "lambda_k1": nrm(ks[6], (DEPTH, DA_HEAD), 0.1),
        "lambda_q2": nrm(ks[7], (DEPTH, DA_HEAD), 0.1),
        "lambda_k2": nrm(ks[8], (DEPTH, DA_HEAD), 0.1),
        "subln_gain": 1.0 + nrm(ks[9], (DEPTH, 2 * DA_HEAD), 0.02),
        "na_rpb": nrm(ks[10], (DEPTH, NA_HEADS, 2 * NA_KH_MAX - 1, 2 * NA_KW - 1), 0.02),
        "w_branch_a": nrm(ks[11], (DEPTH, DA_V, D_MODEL), DA_V ** -0.5),
        "w_branch_b": nrm(ks[12], (DEPTH, NA_W, D_MODEL), NA_W ** -0.5),
        "w_out": nrm(ks[13], (DEPTH, D_MODEL, D_MODEL), D_MODEL ** -0.5),
        "norm_ffn": 1.0 + nrm(ks[14], (DEPTH, D_MODEL), 0.02),
        "peer_wq": nrm(ks[15], (DEPTH, D_MODEL, PEER_HEADS * PEER_DKEY), D_MODEL ** -0.5),
        "peer_keys": nrm(ks[16], (DEPTH, PEER_HEADS, 2, PEER_NKEYS, PEER_HALF), PEER_HALF ** -0.5),
        "peer_u": nrm(ks[17], (DEPTH, PEER_EXPERTS, D_MODEL), D_MODEL ** -0.5),
        "peer_v": nrm(ks[18], (DEPTH, PEER_EXPERTS, D_MODEL), (PEER_HEADS * PEER_TOPK) ** -0.5),
        "norm_final": 1.0 + nrm(ks[19], (D_MODEL,), 0.02),
    }


def reference(x_prompt, x_sample, meta_tokens, norm_mix, w_in, lambda_q1, lambda_k1, lambda_q2,
              lambda_k2, subln_gain, na_rpb, w_branch_a, w_branch_b, w_out, norm_ffn,
              peer_wq, peer_keys, peer_u, peer_v, norm_final):
    y_prompt = _trunk(x_prompt, meta_tokens, norm_mix, w_in, lambda_q1, lambda_k1, lambda_q2, lambda_k2,
                      subln_gain, na_rpb, w_branch_a, w_branch_b, w_out, norm_ffn,
                      peer_wq, peer_keys, peer_u, peer_v, norm_final)
    y_sample = _trunk(x_sample, meta_tokens, norm_mix, w_in, lambda_q1, lambda_k1, lambda_q2, lambda_k2,
                      subln_gain, na_rpb, w_branch_a, w_branch_b, w_out, norm_ffn,
                      peer_wq, peer_keys, peer_u, peer_v, norm_final)
    return (y_prompt, y_sample)
```

```python
import functools
import math

import numpy as np
import jax
import jax.numpy as jnp
from jax import lax
from jax.experimental import pallas as pl
from jax.experimental.pallas import tpu as pltpu

F32 = jnp.float32
BF16 = jnp.bfloat16

D_MODEL = 1024
SEQ = 2048
N_META = 16
GRID_W = 64
ROWS = SEQ // GRID_W
ROPE_THETA = 10000.0
RMS_EPS = 1e-6
DA_HEADS = 4
DA_HEAD = 64
NA_HEADS = 8
NA_HEAD = 64
NA_KH = 8
NA_KW = 16
NA_COLS = 512
PEER_HEADS = 8
PEER_NKEYS = 128
PEER_TOPK = 16
PEER_EXPERTS = PEER_NKEYS * PEER_NKEYS

LANES = 128
TOK = 512
DA_QB = 256
NA_QR = 4
NA_GROUPS = ROWS // NA_QR
NA_UR = NA_QR + NA_KH - 1
NA_QN = NA_QR * GRID_W
NA_UN = NA_UR * GRID_W
PEER_EC = 1024
PEER_NCH = PEER_EXPERTS // PEER_EC
PEER_IPC = PEER_EC // PEER_NKEYS
VMEM_LIMIT = 56 * 1024 * 1024

PEER_CAND = tuple((a, b) for a in range(PEER_TOPK) for b in range(PEER_TOPK)
                  if (a + 1) * (b + 1) <= PEER_TOPK)
NCAND = len(PEER_CAND)

NT_DIMS = (((1,), (1,)), ((), ()))


def _round_up(x, m):
    return (x + m - 1) // m * m


def _cparams(sem):
    return pltpu.CompilerParams(dimension_semantics=sem, vmem_limit_bytes=VMEM_LIMIT)


def _rms(x, gain):
    return x * lax.rsqrt(jnp.mean(x * x, axis=-1, keepdims=True) + RMS_EPS) * gain


def _in_proj_kernel(h_ref, g_ref, w_ref, cos_ref, sin_ref,
                    qda_ref, kda_ref, vda_ref, qna_ref, kna_ref, vna_ref, sga_ref, sgb_ref, xn_ref):
    xn_ref[...] = _rms(h_ref[...], g_ref[...]).astype(BF16)

    def mm(c0, n):
        return jnp.dot(xn_ref[...], w_ref[:, c0:c0 + n], preferred_element_type=F32)

    cos = cos_ref[...]
    sin = sin_ref[...]
    qda_ref[...] = ((mm(0, 512) * cos + mm(512, 512) * sin) * (DA_HEAD ** -0.5)).astype(BF16)
    kda_ref[...] = (mm(1024, 512) * cos + mm(1536, 512) * sin).astype(BF16)
    vda_ref[...] = mm(2048, 512).astype(BF16)
    qna_ref[...] = (mm(2560, 512) * (NA_HEAD ** -0.5)).astype(BF16)
    kna_ref[...] = mm(3072, 512).astype(BF16)
    vna_ref[...] = mm(3584, 512).astype(BF16)
    sga_ref[...] = jax.nn.sigmoid(mm(4096, 1024)).astype(BF16)
    sgb_ref[...] = jax.nn.sigmoid(mm(5120, 1024)).astype(BF16)


def _in_proj(h, gains, w_ext, cos_t, sin_t, layer, n_real_blocks):
    T = h.shape[0]
    per_seq = SEQ // TOK

    def row(i):
        return (i, 0)

    def tbl(i):
        return (jnp.where(i < n_real_blocks, i % per_seq, per_seq), 0)

    outs = [jax.ShapeDtypeStruct((T, 512), BF16)] * 6 + [jax.ShapeDtypeStruct((T, D_MODEL), BF16)] * 2
    return pl.pallas_call(
        _in_proj_kernel,
        grid=(T // TOK,),
        in_specs=[
            pl.BlockSpec((TOK, D_MODEL), row),
            pl.BlockSpec((None, 1, D_MODEL), lambda i: (layer, 0, 0)),
            pl.BlockSpec((None, D_MODEL, 6144), lambda i: (layer, 0, 0)),
            pl.BlockSpec((TOK, 512), tbl),
            pl.BlockSpec((TOK, 512), tbl),
        ],
        out_specs=[pl.BlockSpec((TOK, 512), row)] * 6 + [pl.BlockSpec((TOK, D_MODEL), row)] * 2,
        out_shape=outs,
        scratch_shapes=[pltpu.VMEM((TOK, D_MODEL), BF16)],
        compiler_params=_cparams(("parallel",)),
        name="in_proj",
    )(h, gains, w_ext, cos_t, sin_t)


def _lambda(lq1_ref, lk1_ref, lq2_ref, lk2_ref, lam_init):
    a = jnp.sum(lq1_ref[...] * lk1_ref[...], axis=-1, keepdims=True)
    b = jnp.sum(lq2_ref[...] * lk2_ref[...], axis=-1, keepdims=True)
    return jnp.exp(a) - jnp.exp(b) + lam_init


def _split_halves(x):
    lane = lax.broadcasted_iota(jnp.int32, x.shape, 1)
    zero = jnp.zeros_like(x)
    return jnp.concatenate([jnp.where(lane < 64, x, zero), jnp.where(lane >= 64, x, zero)], axis=0)


def _diff_attend(q, kr_ref, vr_ref, km_ref, vm_ref, lam, gain, lam_init, o_ref):
    n = q.shape[0]
    for h in range(DA_HEADS):
        sl = slice(h * 128, (h + 1) * 128)
        qq = _split_halves(q[:, sl])
        s = lax.dot_general(qq, kr_ref[:, sl], NT_DIMS, preferred_element_type=F32)
        sm = lax.dot_general(qq, km_ref[:, sl], NT_DIMS, preferred_element_type=F32)
        m = jnp.maximum(jnp.max(s, axis=-1, keepdims=True), jnp.max(sm, axis=-1, keepdims=True))
        p = jnp.exp(s - m)
        pm = jnp.exp(sm - m)
        inv = 1.0 / (jnp.sum(p, axis=-1, keepdims=True) + jnp.sum(pm, axis=-1, keepdims=True))
        c1 = inv[:n]
        c2 = inv[n:] * lam
        a = (p[:n] * c1 - p[n:] * c2).astype(BF16)
        am = (pm[:n] * c1 - pm[n:] * c2).astype(BF16)
        o = (jnp.dot(a, vr_ref[:, sl], preferred_element_type=F32)
             + jnp.dot(am, vm_ref[:, sl], preferred_element_type=F32))
        o = _rms(o, gain) * (1.0 - lam_init)
        o_ref[:, sl] = o.astype(BF16)


def _da_kernel(q_ref, kr_ref, vr_ref, km_ref, vm_ref, lq1_ref, lk1_ref, lq2_ref, lk2_ref, sg_ref,
               o_ref, *, lam_init):
    lam = _lambda(lq1_ref, lk1_ref, lq2_ref, lk2_ref, lam_init)
    _diff_attend(q_ref[...], kr_ref, vr_ref, km_ref, vm_ref, lam, sg_ref[...], lam_init, o_ref)


def _lam_specs(layer):
    return [pl.BlockSpec((None, 1, DA_HEAD), lambda *_: (layer, 0, 0))] * 4 + \
           [pl.BlockSpec((None, 1, 2 * DA_HEAD), lambda *_: (layer, 0, 0))]


def _diff_attention(qda, kda, vda, lams, subln, layer, lam_init, nb, T):
    R = nb * SEQ
    qpb = SEQ // DA_QB
    meta_blk0 = R // N_META
    return pl.pallas_call(
        functools.partial(_da_kernel, lam_init=lam_init),
        grid=(nb, qpb),
        in_specs=[
            pl.BlockSpec((DA_QB, 512), lambda b, j: (b * qpb + j, 0)),
            pl.BlockSpec((SEQ, 512), lambda b, j: (b, 0)),
            pl.BlockSpec((SEQ, 512), lambda b, j: (b, 0)),
            pl.BlockSpec((N_META, 512), lambda b, j: (meta_blk0 + b, 0)),
            pl.BlockSpec((N_META, 512), lambda b, j: (meta_blk0 + b, 0)),
        ] + _lam_specs(layer),
        out_specs=pl.BlockSpec((DA_QB, 512), lambda b, j: (b * qpb + j, 0)),
        out_shape=jax.ShapeDtypeStruct((T, 512), BF16),
        compiler_params=_cparams(("parallel", "parallel")),
        name="diff_attn",
    )(qda, kda, vda, kda, vda, *lams, subln)


def _na_union_start(g):
    return min(max(NA_QR * g - NA_KH // 2, 0), ROWS - NA_KH, ROWS - NA_UR)


def _na_table_id(g):
    return 0 if g == 0 else (2 if g == NA_GROUPS - 1 else 1)


def _na_bias_tables(rpb):
    tabs = []
    for g in (0, 1, NA_GROUPS - 1):
        us = _na_union_start(g)
        qi = np.arange(NA_QN)
        ki = np.arange(NA_UN)
        r = NA_QR * g + qi // GRID_W
        qc = qi % GRID_W
        kr = us + ki // GRID_W
        kc = ki % GRID_W
        rs = np.clip(r - NA_KH // 2, 0, ROWS - NA_KH)
        cs = np.clip(qc - NA_KW // 2, 0, GRID_W - NA_KW)
        valid = ((kr[None, :] >= rs[:, None]) & (kr[None, :] < rs[:, None] + NA_KH)
                 & (kc[None, :] >= cs[:, None]) & (kc[None, :] < cs[:, None] + NA_KW))
        dr = np.clip(kr[None, :] - r[:, None] + (NA_KH - 1), 0, 2 * NA_KH - 2)
        dc = np.clip(kc[None, :] - qc[:, None], -(NA_KW - 1), NA_KW - 1) + (NA_KW - 1)
        t = rpb.astype(F32)[:, :, dr, dc]
        tabs.append(jnp.where(jnp.asarray(valid)[None, None], t, -jnp.inf))
    return jnp.stack(tabs, axis=1)


def _na_kernel(q_ref, k_ref, v_ref, km_ref, vm_ref, bias_ref, o_ref):
    g = pl.program_id(0)
    us = jnp.minimum(jnp.clip(NA_QR * g - NA_KH // 2, 0, ROWS - NA_KH), ROWS - NA_UR)
    start = pl.multiple_of(us * GRID_W, GRID_W)
    for hp in range(NA_HEADS // 2):
        sl = slice(hp * 128, (hp + 1) * 128)
        qq = _split_halves(q_ref[:, sl])
        kw = k_ref[pl.ds(start, NA_UN), sl]
        vw = v_ref[pl.ds(start, NA_UN), sl]
        s = lax.dot_general(qq, kw, NT_DIMS, preferred_element_type=F32)
        s = s + jnp.concatenate([bias_ref[2 * hp], bias_ref[2 * hp + 1]], axis=0)
        sm = lax.dot_general(qq, km_ref[:, sl], NT_DIMS, preferred_element_type=F32)
        m = jnp.maximum(jnp.max(s, axis=-1, keepdims=True), jnp.max(sm, axis=-1, keepdims=True))
        p = jnp.exp(s - m)
        pm = jnp.exp(sm - m)
        inv = 1.0 / (jnp.sum(p, axis=-1, keepdims=True) + jnp.sum(pm, axis=-1, keepdims=True))
        o2 = (jnp.dot((p * inv).astype(BF16), vw, preferred_element_type=F32)
              + jnp.dot((pm * inv).astype(BF16), vm_ref[:, sl], preferred_element_type=F32))
        lane = lax.broadcasted_iota(jnp.int32, (NA_QN, 128), 1)
        o_ref[:, sl] = jnp.where(lane < 64, o2[:NA_QN], o2[NA_QN:]).astype(BF16)


def _na_attention(qna, kna, vna, bias, layer, nb, T):
    R = nb * SEQ
    meta_blk0 = R // N_META
    last = NA_GROUPS - 1

    def tid(g, b):
        return (layer, jnp.where(g == 0, 0, jnp.where(g == last, 2, 1)), 0, 0, 0)

    return pl.pallas_call(
        _na_kernel,
        grid=(NA_GROUPS, nb),
        in_specs=[
            pl.BlockSpec((NA_QN, NA_COLS), lambda g, b: (b * NA_GROUPS + g, 0)),
            pl.BlockSpec((SEQ, NA_COLS), lambda g, b: (b, 0)),
            pl.BlockSpec((SEQ, NA_COLS), lambda g, b: (b, 0)),
            pl.BlockSpec((N_META, NA_COLS), lambda g, b: (meta_blk0 + b, 0)),
            pl.BlockSpec((N_META, NA_COLS), lambda g, b: (meta_blk0 + b, 0)),
            pl.BlockSpec((None, None, NA_HEADS, NA_QN, NA_UN), tid),
        ],
        out_specs=pl.BlockSpec((NA_QN, NA_COLS), lambda g, b: (b * NA_GROUPS + g, 0)),
        out_shape=jax.ShapeDtypeStruct((T, NA_COLS), BF16),
        compiler_params=_cparams(("parallel", "parallel")),
        name="nbhd_attn",
    )(qna, kna, vna, kna, vna, bias)


def _meta_kernel(qd_ref, kr_ref, vr_ref, kdm_ref, vdm_ref, qn_ref, knm_ref, vnm_ref,
                 lq1_ref, lk1_ref, lq2_ref, lk2_ref, sg_ref, oda_in, ona_in, oda_ref, ona_ref,
                 *, lam_init, nb):
    del oda_in, ona_in
    i = pl.program_id(0)

    @pl.when(i < nb)
    def _():
        lam = _lambda(lq1_ref, lk1_ref, lq2_ref, lk2_ref, lam_init)
        _diff_attend(qd_ref[...], kr_ref, vr_ref, kdm_ref, vdm_ref, lam, sg_ref[...], lam_init, oda_ref)
        for hp in range(NA_HEADS // 2):
            sl = slice(hp * 128, (hp + 1) * 128)
            qq = _split_halves(qn_ref[:, sl])
            s = lax.dot_general(qq, knm_ref[:, sl], NT_DIMS, preferred_element_type=F32)
            p = jnp.exp(s - jnp.max(s, axis=-1, keepdims=True))
            p = p * (1.0 / jnp.sum(p, axis=-1, keepdims=True))
            o2 = jnp.dot(p.astype(BF16), vnm_ref[:, sl], preferred_element_type=F32)
            lane = lax.broadcasted_iota(jnp.int32, (N_META, 128), 1)
            ona_ref[:, sl] = jnp.where(lane < 64, o2[:N_META], o2[N_META:]).astype(BF16)

    @pl.when(i >= nb)
    def _():
        oda_ref[...] = jnp.zeros_like(oda_ref)
        ona_ref[...] = jnp.zeros_like(ona_ref)


def _meta_queries(qda, kda, vda, qna, kna, vna, lams, subln, o_da, o_na, layer, lam_init, nb, T):
    R = nb * SEQ
    meta_blk0 = R // N_META
    n_steps = (T - R) // N_META

    def seq(i):
        return (jnp.minimum(i, nb - 1), 0)

    def meta_in(i):
        return (meta_blk0 + jnp.minimum(i, nb - 1), 0)

    def meta_out(i):
        return (meta_blk0 + i, 0)

    mspec = pl.BlockSpec((N_META, 512), meta_in)
    return pl.pallas_call(
        functools.partial(_meta_kernel, lam_init=lam_init, nb=nb),
        grid=(n_steps,),
        in_specs=[mspec, pl.BlockSpec((SEQ, 512), seq), pl.BlockSpec((SEQ, 512), seq), mspec, mspec,
                  mspec, mspec, mspec] + _lam_specs(layer) +
                 [pl.BlockSpec(memory_space=pl.ANY), pl.BlockSpec(memory_space=pl.ANY)],
        out_specs=[pl.BlockSpec((N_META, 512), meta_out), pl.BlockSpec((N_META, 512), meta_out)],
        out_shape=[jax.ShapeDtypeStruct((T, 512), BF16), jax.ShapeDtypeStruct((T, 512), BF16)],
        input_output_aliases={13: 0, 14: 1},
        compiler_params=_cparams(("arbitrary",)),
        name="meta_queries",
    )(qda, kda, vda, kda, vda, qna, kna, vna, *lams, subln, o_da, o_na)


def _merge_kernel(oda_ref, ona_ref, sga_ref, sgb_ref, h_ref, wa_ref, wb_ref, wo_ref, out_ref):
    ya = jnp.dot(oda_ref[...], wa_ref[...], preferred_element_type=F32)
    yb = jnp.dot(ona_ref[...], wb_ref[...], preferred_element_type=F32)
    merged = sga_ref[...].astype(F32) * ya + sgb_ref[...].astype(F32) * yb
    out_ref[...] = h_ref[...] + jnp.dot(merged.astype(BF16), wo_ref[...], preferred_element_type=F32)


def _merge(o_da, o_na, sga, sgb, h, wa, wb, wo, layer):
    T = h.shape[0]

    def row(i):
        return (i, 0)

    def lyr(i):
        return (layer, 0, 0)

    return pl.pallas_call(
        _merge_kernel,
        grid=(T // TOK,),
        in_specs=[
            pl.BlockSpec((TOK, 512), row), pl.BlockSpec((TOK, 512), row),
            pl.BlockSpec((TOK, D_MODEL), row), pl.BlockSpec((TOK, D_MODEL), row),
            pl.BlockSpec((TOK, D_MODEL), row),
            pl.BlockSpec((None, 512, D_MODEL), lyr), pl.BlockSpec((None, 512, D_MODEL), lyr),
            pl.BlockSpec((None, D_MODEL, D_MODEL), lyr),
        ],
        out_specs=pl.BlockSpec((TOK, D_MODEL), row),
        out_shape=jax.ShapeDtypeStruct((T, D_MODEL), F32),
        input_output_aliases={4: 0},
        compiler_params=_cparams(("parallel",)),
        name="merge",
    )(o_da, o_na, sga, sgb, h, wa, wb, wo)


def _peer_kernel(h_ref, g_ref, wqT_ref, keys_ref, u_ref, vT_ref, out_ref,
                 cT_ref, qT_ref, s_ref, rk_ref, r1_ref, r2_ref, w1_ref, p2_ref, cnti_ref,
                 s1s_ref, s2s_ref, cnt_ref, zinv_ref, hid_ref, wT_ref, acc_ref):
    j = pl.program_id(1)
    ntile = TOK // LANES

    @pl.when(j == 0)
    def _route():
        c = _rms(h_ref[...], g_ref[...])
        cT_ref[...] = c.T.astype(BF16)
        qT_ref[...] = jnp.dot(wqT_ref[...], cT_ref[...], preferred_element_type=F32).astype(BF16)
        kiota = lax.broadcasted_iota(jnp.int32, (PEER_NKEYS, TOK), 0).astype(F32)

        for h in range(PEER_HEADS):
            for part in range(2):
                rows = slice((2 * h + part) * 128, (2 * h + part + 1) * 128)
                s = jnp.dot(keys_ref[h, part], qT_ref[rows, :], preferred_element_type=F32)
                s_ref[...] = s
                rk_ref[...] = jnp.full((PEER_NKEYS, TOK), float(PEER_TOPK), F32)
                sorted_ref = s1s_ref if part == 0 else s2s_ref

                def extract(a, carry):
                    sv = s_ref[...]
                    m = jnp.max(sv, axis=0, keepdims=True)
                    first = jnp.min(jnp.where(sv == m, kiota, float(PEER_NKEYS)), axis=0, keepdims=True)
                    hit = kiota == first
                    s_ref[...] = jnp.where(hit, -jnp.inf, sv)
                    rk_ref[...] = jnp.where(hit, a.astype(F32), rk_ref[...])
                    sorted_ref[a, pl.ds(h, 1), :] = m
                    return carry

                lax.fori_loop(0, PEER_TOPK, extract, 0)
                e = jnp.exp(s - sorted_ref[0, pl.ds(h, 1), :])
                if part == 0:
                    r1_ref[h] = rk_ref[...]
                    w1_ref[h] = e
                else:
                    r2_ref[h] = rk_ref[...]
                    p2_ref[h] = e

        def joint(t, carry):
            ls = pl.ds(pl.multiple_of(t * LANES, LANES), LANES)
            s1 = [s1s_ref[a, :, ls] for a in range(PEER_TOPK)]
            s2 = [s2s_ref[b, :, ls] for b in range(PEER_TOPK)]
            cs = [s1[a] + s2[b] for (a, b) in PEER_CAND]
            ahead = [jnp.zeros((PEER_HEADS, LANES), F32) for _ in range(NCAND)]
            behind = [jnp.zeros((PEER_HEADS, LANES), F32) for _ in range(NCAND)]
            for c in range(NCAND):
                for c2 in range(c):
                    w = jnp.where(cs[c2] >= cs[c], 1.0, 0.0)
                    ahead[c] = ahead[c] + w
                    behind[c2] = behind[c2] + w
            e1 = [jnp.exp(s1[a] - s1[0]) for a in range(PEER_TOPK)]
            e2 = [jnp.exp(s2[b] - s2[0]) for b in range(PEER_TOPK)]
            z = jnp.zeros((PEER_HEADS, LANES), F32)
            cnt = [jnp.zeros((PEER_HEADS, LANES), F32) for _ in range(PEER_TOPK)]
            for c, (a, b) in enumerate(PEER_CAND):
                rank = ahead[c] + (float(NCAND - 1 - c) - behind[c])
                sel = jnp.where(rank < float(PEER_TOPK), 1.0, 0.0)
                cnt[a] = cnt[a] + sel
                z = z + sel * (e1[a] * e2[b])
            for a in range(PEER_TOPK):
                cnt_ref[a, :, ls] = cnt[a]
            zinv_ref[:, ls] = 1.0 / z
            return carry

        lax.fori_loop(0, ntile, joint, 0)

        for h in range(PEER_HEADS):
            r1 = r1_ref[h]
            ci = jnp.zeros((PEER_NKEYS, TOK), F32)
            for a in range(PEER_TOPK):
                ci = jnp.where(r1 == float(a), cnt_ref[a, pl.ds(h, 1), :], ci)
            cnti_ref[h] = ci
            w1_ref[h] = w1_ref[h] * zinv_ref[pl.ds(h, 1), :]

    hid_ref[...] = jnp.dot(u_ref[...], cT_ref[...], preferred_element_type=F32)

    i0 = pl.multiple_of(j * PEER_IPC, PEER_IPC)

    def gate_tile(t, carry):
        ls = pl.ds(pl.multiple_of(t * LANES, LANES), LANES)
        cnt8 = [cnti_ref[h, pl.ds(i0, PEER_IPC), ls] for h in range(PEER_HEADS)]
        w8 = [w1_ref[h, pl.ds(i0, PEER_IPC), ls] for h in range(PEER_HEADS)]
        for ii in range(PEER_IPC):
            rows = slice(ii * PEER_NKEYS, (ii + 1) * PEER_NKEYS)
            g = jnp.zeros((PEER_NKEYS, LANES), F32)
            for h in range(PEER_HEADS):
                g = g + jnp.where(r2_ref[h, :, ls] < cnt8[h][ii:ii + 1],
                                  p2_ref[h, :, ls] * w8[h][ii:ii + 1], 0.0)
            hid = hid_ref[rows, ls]
            act = 0.5 * hid * (1.0 + lax.erf(hid * math.sqrt(0.5)))
            wT_ref[rows, ls] = (g * act).astype(BF16)
        return carry

    lax.fori_loop(0, ntile, gate_tile, 0)

    contrib = jnp.dot(vT_ref[...], wT_ref[...], preferred_element_type=F32)

    @pl.when(j == 0)
    def _():
        acc_ref[...] = contrib

    @pl.when(j > 0)
    def _():
        acc_ref[...] += contrib

    @pl.when(j == PEER_NCH - 1)
    def _():
        out_ref[...] = h_ref[...] + acc_ref[...].T


def _peer(h, gains, wqT, keys, u, vT, layer):
    T = h.shape[0]
    hk = (PEER_HEADS, PEER_NKEYS, TOK)
    return pl.pallas_call(
        _peer_kernel,
        grid=(T // TOK, PEER_NCH),
        in_specs=[
            pl.BlockSpec((TOK, D_MODEL), lambda t, j: (t, 0)),
            pl.BlockSpec((None, 1, D_MODEL), lambda t, j: (layer, 0, 0)),
            pl.BlockSpec((None, PEER_HEADS * 256, D_MODEL), lambda t, j: (layer, 0, 0)),
            pl.BlockSpec((None, PEER_HEADS, 2, PEER_NKEYS, 128), lambda t, j: (layer, 0, 0, 0, 0)),
            pl.BlockSpec((None, PEER_EC, D_MODEL), lambda t, j: (layer, j, 0)),
            pl.BlockSpec((None, D_MODEL, PEER_EC), lambda t, j: (layer, 0, j)),
        ],
        out_specs=pl.BlockSpec((TOK, D_MODEL), lambda t, j: (t, 0)),
        out_shape=jax.ShapeDtypeStruct((T, D_MODEL), F32),
        scratch_shapes=[
            pltpu.VMEM((D_MODEL, TOK), BF16),
            pltpu.VMEM((PEER_HEADS * 256, TOK), BF16),
            pltpu.VMEM((PEER_NKEYS, TOK), F32),
            pltpu.VMEM((PEER_NKEYS, TOK), F32),
            pltpu.VMEM(hk, F32),
            pltpu.VMEM(hk, F32),
            pltpu.VMEM(hk, F32),
            pltpu.VMEM(hk, F32),
            pltpu.VMEM(hk, F32),
            pltpu.VMEM((PEER_TOPK, PEER_HEADS, TOK), F32),
            pltpu.VMEM((PEER_TOPK, PEER_HEADS, TOK), F32),
            pltpu.VMEM((PEER_TOPK, PEER_HEADS, TOK), F32),
            pltpu.VMEM((PEER_HEADS, TOK), F32),
            pltpu.VMEM((PEER_EC, TOK), F32),
            pltpu.VMEM((PEER_EC, TOK), BF16),
            pltpu.VMEM((D_MODEL, TOK), F32),
        ],
        input_output_aliases={0: 0},
        compiler_params=_cparams(("parallel", "arbitrary")),
        name="peer",
    )(h, gains, wqT, keys, u, vT)


def _final_kernel(h_ref, g_ref, o_ref):
    o_ref[...] = _rms(h_ref[...], g_ref[...])


def _final_norm(h, gain, blk0, nblk):
    return pl.pallas_call(
        _final_kernel,
        grid=(nblk,),
        in_specs=[pl.BlockSpec((TOK, D_MODEL), lambda i: (blk0 + i, 0)),
                  pl.BlockSpec((1, D_MODEL), lambda i: (0, 0))],
        out_specs=pl.BlockSpec((TOK, D_MODEL), lambda i: (i, 0)),
        out_shape=jax.ShapeDtypeStruct((nblk * TOK, D_MODEL), F32),
        compiler_params=_cparams(("parallel",)),
        name="final_norm",
    )(h, gain)


def _rope_tables():
    half = DA_HEAD // 2
    inv = 1.0 / (ROPE_THETA ** (jnp.arange(half, dtype=F32) * 2.0 / DA_HEAD))
    pos = jnp.concatenate([jnp.arange(N_META, N_META + SEQ, dtype=F32),
                           jnp.tile(jnp.arange(N_META, dtype=F32), TOK // N_META)])
    ang = pos[:, None] * inv[None, :]
    cos, sin = jnp.cos(ang), jnp.sin(ang)
    cos_t = jnp.tile(cos, (1, 512 // half))
    sin_t = jnp.tile(jnp.concatenate([-sin, sin], axis=1), (1, 512 // DA_HEAD))
    return cos_t, sin_t


def kernel(x_prompt, x_sample, meta_tokens, norm_mix, w_in, lambda_q1, lambda_k1, lambda_q2, lambda_k2,
           subln_gain, na_rpb, w_branch_a, w_branch_b, w_out, norm_ffn, peer_wq, peer_keys, peer_u,
           peer_v, norm_final):
    depth = w_in.shape[0]
    nb1, nb2 = x_prompt.shape[0], x_sample.shape[0]
    nb = nb1 + nb2
    R = nb * SEQ
    T = R + _round_up(nb * N_META, TOK)

    h = jnp.concatenate([
        x_prompt.reshape(nb1 * SEQ, D_MODEL), x_sample.reshape(nb2 * SEQ, D_MODEL),
        jnp.broadcast_to(meta_tokens[None], (nb, N_META, D_MODEL)).reshape(nb * N_META, D_MODEL),
        jnp.zeros((T - R - nb * N_META, D_MODEL), F32)], axis=0)

    swap = np.array([(c // DA_HEAD) * DA_HEAD + (c % DA_HEAD + DA_HEAD // 2) % DA_HEAD for c in range(512)])
    wq, wk = w_in[:, :, :512], w_in[:, :, 512:1024]
    w_ext = jnp.concatenate([wq, wq[:, :, swap], wk, wk[:, :, swap], w_in[:, :, 1024:]], axis=2).astype(BF16)
    cos_t, sin_t = _rope_tables()
    bias = _na_bias_tables(na_rpb)
    wa, wb, wo = w_branch_a.astype(BF16), w_branch_b.astype(BF16), w_out.astype(BF16)
    wqT = jnp.transpose(peer_wq, (0, 2, 1)).astype(BF16)
    keys = peer_keys.astype(BF16)
    u = peer_u.astype(BF16)
    vT = jnp.transpose(peer_v, (0, 2, 1)).astype(BF16)
    lams = [x.reshape(depth, 1, DA_HEAD) for x in (lambda_q1, lambda_k1, lambda_q2, lambda_k2)]
    subln = subln_gain.reshape(depth, 1, 2 * DA_HEAD)
    g_mix = norm_mix.reshape(depth, 1, D_MODEL)
    g_ffn = norm_ffn.reshape(depth, 1, D_MODEL)

    for l in range(depth):
        lam_init = 0.8 - 0.6 * math.exp(-0.3 * l)
        qda, kda, vda, qna, kna, vna, sga, sgb = _in_proj(h, g_mix, w_ext, cos_t, sin_t, l, R // TOK)
        o_da = _diff_attention(qda, kda, vda, lams, subln, l, lam_init, nb, T)
        o_na = _na_attention(qna, kna, vna, bias, l, nb, T)
        o_da, o_na = _meta_queries(qda, kda, vda, qna, kna, vna, lams, subln, o_da, o_na, l, lam_init, nb, T)
        h = _merge(o_da, o_na, sga, sgb, h, wa, wb, wo, l)
        h = _peer(h, g_ffn, wqT, keys, u, vT, l)

    gf = norm_final.reshape(1, D_MODEL)
    y1 = _final_norm(h, gf, 0, nb1 * SEQ // TOK).reshape(nb1, SEQ, D_MODEL)
    y2 = _final_norm(h, gf, nb1 * SEQ // TOK, nb2 * SEQ // TOK).reshape(nb2, SEQ, D_MODEL)
    return (y1, y2)
```

```python
import functools
import math

import numpy as np
import jax
import jax.numpy as jnp
from jax import lax
from jax.experimental import pallas as pl
from jax.experimental.pallas import tpu as pltpu

F32 = jnp.float32
BF16 = jnp.bfloat16

D_MODEL = 1024
SEQ = 2048
N_META = 16
GRID_W = 64
ROWS = SEQ // GRID_W
ROPE_THETA = 10000.0
RMS_EPS = 1e-6
DA_HEADS = 4
DA_HEAD = 64
NA_HEADS = 8
NA_HEAD = 64
NA_KH = 8
NA_KW = 16
NA_COLS = 512
PEER_HEADS = 8
PEER_NKEYS = 128
PEER_TOPK = 16
PEER_EXPERTS = PEER_NKEYS * PEER_NKEYS

LANES = 128
TOK = 512
DA_QB = 256
NA_QR = 4
NA_GROUPS = ROWS // NA_QR
NA_UR = NA_QR + NA_KH - 1
NA_QN = NA_QR * GRID_W
NA_UN = NA_UR * GRID_W
PEER_EC = 1024
PEER_NCH = PEER_EXPERTS // PEER_EC
PEER_NSTEP = PEER_NCH // 2
PEER_IPC = PEER_EC // PEER_NKEYS
GATE_KEYS = 128
GATE_IG = 2
VMEM_LIMIT = 60 * 1024 * 1024

PEER_CAND = tuple((a, b) for a in range(PEER_TOPK) for b in range(PEER_TOPK)
                  if (a + 1) * (b + 1) <= PEER_TOPK)
NCAND = len(PEER_CAND)

NT_DIMS = (((1,), (1,)), ((), ()))


def _round_up(x, m):
    return (x + m - 1) // m * m


def _cparams(sem):
    return pltpu.CompilerParams(dimension_semantics=sem, vmem_limit_bytes=VMEM_LIMIT)


def _rms(x, gain):
    return x * lax.rsqrt(jnp.mean(x * x, axis=-1, keepdims=True) + RMS_EPS) * gain


def _in_proj_kernel(h_ref, g_ref, w_ref, cos_ref, sin_ref,
                    qda_ref, kda_ref, vda_ref, qna_ref, kna_ref, vna_ref, sga_ref, sgb_ref, xn_ref):
    xn_ref[...] = _rms(h_ref[...], g_ref[...]).astype(BF16)

    def mm(c0, n):
        return jnp.dot(xn_ref[...], w_ref[:, c0:c0 + n], preferred_element_type=F32)

    cos = cos_ref[...]
    sin = sin_ref[...]
    qda_ref[...] = ((mm(0, 512) * cos + mm(512, 512) * sin) * (DA_HEAD ** -0.5)).astype(BF16)
    kda_ref[...] = (mm(1024, 512) * cos + mm(1536, 512) * sin).astype(BF16)
    vda_ref[...] = mm(2048, 512).astype(BF16)
    qna_ref[...] = (mm(2560, 512) * (NA_HEAD ** -0.5)).astype(BF16)
    kna_ref[...] = mm(3072, 512).astype(BF16)
    vna_ref[...] = mm(3584, 512).astype(BF16)
    sga_ref[...] = jax.nn.sigmoid(mm(4096, 1024)).astype(BF16)
    sgb_ref[...] = jax.nn.sigmoid(mm(5120, 1024)).astype(BF16)


def _in_proj(h, gains, w_ext, cos_t, sin_t, layer, n_real_blocks):
    T = h.shape[0]
    per_seq = SEQ // TOK

    def row(i):
        return (i, 0)

    def tbl(i):
        return (jnp.where(i < n_real_blocks, i % per_seq, per_seq), 0)

    outs = [jax.ShapeDtypeStruct((T, 512), BF16)] * 6 + [jax.ShapeDtypeStruct((T, D_MODEL), BF16)] * 2
    return pl.pallas_call(
        _in_proj_kernel,
        grid=(T // TOK,),
        in_specs=[
            pl.BlockSpec((TOK, D_MODEL), row),
            pl.BlockSpec((None, 1, D_MODEL), lambda i: (layer, 0, 0)),
            pl.BlockSpec((None, D_MODEL, 6144), lambda i: (layer, 0, 0)),
            pl.BlockSpec((TOK, 512), tbl),
            pl.BlockSpec((TOK, 512), tbl),
        ],
        out_specs=[pl.BlockSpec((TOK, 512), row)] * 6 + [pl.BlockSpec((TOK, D_MODEL), row)] * 2,
        out_shape=outs,
        scratch_shapes=[pltpu.VMEM((TOK, D_MODEL), BF16)],
        compiler_params=_cparams(("parallel",)),
        name="in_proj",
    )(h, gains, w_ext, cos_t, sin_t)


def _lambda(lq1_ref, lk1_ref, lq2_ref, lk2_ref, lam_init):
    a = jnp.sum(lq1_ref[...] * lk1_ref[...], axis=-1, keepdims=True)
    b = jnp.sum(lq2_ref[...] * lk2_ref[...], axis=-1, keepdims=True)
    return jnp.exp(a) - jnp.exp(b) + lam_init


def _split_halves(x):
    lane = lax.broadcasted_iota(jnp.int32, x.shape, 1)
    zero = jnp.zeros_like(x)
    return jnp.concatenate([jnp.where(lane < 64, x, zero), jnp.where(lane >= 64, x, zero)], axis=0)


def _diff_attend(q, kr_ref, vr_ref, km_ref, vm_ref, lam, gain, lam_init, o_ref):
    n = q.shape[0]
    for h in range(DA_HEADS):
        sl = slice(h * 128, (h + 1) * 128)
        qq = _split_halves(q[:, sl])
        s = lax.dot_general(qq, kr_ref[:, sl], NT_DIMS, preferred_element_type=F32)
        sm = lax.dot_general(qq, km_ref[:, sl], NT_DIMS, preferred_element_type=F32)
        m = jnp.maximum(jnp.max(s, axis=-1, keepdims=True), jnp.max(sm, axis=-1, keepdims=True))
        p = jnp.exp(s - m)
        pm = jnp.exp(sm - m)
        inv = 1.0 / (jnp.sum(p, axis=-1, keepdims=True) + jnp.sum(pm, axis=-1, keepdims=True))
        c1 = inv[:n]
        c2 = inv[n:] * lam
        a = (p[:n] * c1 - p[n:] * c2).astype(BF16)
        am = (pm[:n] * c1 - pm[n:] * c2).astype(BF16)
        o = (jnp.dot(a, vr_ref[:, sl], preferred_element_type=F32)
             + jnp.dot(am, vm_ref[:, sl], preferred_element_type=F32))
        o = _rms(o, gain) * (1.0 - lam_init)
        o_ref[:, sl] = o.astype(BF16)


def _da_kernel(q_ref, kr_ref, vr_ref, km_ref, vm_ref, lq1_ref, lk1_ref, lq2_ref, lk2_ref, sg_ref,
               o_ref, *, lam_init):
    lam = _lambda(lq1_ref, lk1_ref, lq2_ref, lk2_ref, lam_init)
    _diff_attend(q_ref[...], kr_ref, vr_ref, km_ref, vm_ref, lam, sg_ref[...], lam_init, o_ref)


def _lam_specs(layer):
    return [pl.BlockSpec((None, 1, DA_HEAD), lambda *_: (layer, 0, 0))] * 4 + \
           [pl.BlockSpec((None, 1, 2 * DA_HEAD), lambda *_: (layer, 0, 0))]


def _diff_attention(qda, kda, vda, lams, subln, layer, lam_init, nb, T):
    R = nb * SEQ
    qpb = SEQ // DA_QB
    meta_blk0 = R // N_META
    return pl.pallas_call(
        functools.partial(_da_kernel, lam_init=lam_init),
        grid=(nb, qpb),
        in_specs=[
            pl.BlockSpec((DA_QB, 512), lambda b, j: (b * qpb + j, 0)),
            pl.BlockSpec((SEQ, 512), lambda b, j: (b, 0)),
            pl.BlockSpec((SEQ, 512), lambda b, j: (b, 0)),
            pl.BlockSpec((N_META, 512), lambda b, j: (meta_blk0 + b, 0)),
            pl.BlockSpec((N_META, 512), lambda b, j: (meta_blk0 + b, 0)),
        ] + _lam_specs(layer),
        out_specs=pl.BlockSpec((DA_QB, 512), lambda b, j: (b * qpb + j, 0)),
        out_shape=jax.ShapeDtypeStruct((T, 512), BF16),
        compiler_params=_cparams(("parallel", "parallel")),
        name="diff_attn",
    )(qda, kda, vda, kda, vda, *lams, subln)


def _na_union_start(g):
    return min(max(NA_QR * g - NA_KH // 2, 0), ROWS - NA_KH, ROWS - NA_UR)


def _na_table_id(g):
    return 0 if g == 0 else (2 if g == NA_GROUPS - 1 else 1)


def _na_bias_tables(rpb):
    depth = rpb.shape[0]
    qc = np.arange(GRID_W)
    kc = np.arange(GRID_W)
    dc = np.clip(kc[None, :] - qc[:, None], -(NA_KW - 1), NA_KW - 1) + (NA_KW - 1)
    onehot = (dc[None] == np.arange(2 * NA_KW - 1)[:, None, None]).astype(np.float32)
    cs = np.clip(qc - NA_KW // 2, 0, GRID_W - NA_KW)
    col_ok = (kc[None, :] >= cs[:, None]) & (kc[None, :] < cs[:, None] + NA_KW)
    t = jnp.einsum("lhrc,cqk->lhrqk", rpb.astype(F32), jnp.asarray(onehot), precision=lax.Precision.HIGHEST)
    t = jnp.where(jnp.asarray(col_ok), t, -jnp.inf)
    neg = jnp.full((depth, NA_HEADS, GRID_W, GRID_W), -jnp.inf, F32)
    tabs = []
    for g in (0, 1, NA_GROUPS - 1):
        us = _na_union_start(g)
        q_rows = []
        for ri in range(NA_QR):
            r = NA_QR * g + ri
            rs = min(max(r - NA_KH // 2, 0), ROWS - NA_KH)
            blocks = [t[:, :, us + ki - r + NA_KH - 1] if rs <= us + ki < rs + NA_KH else neg
                      for ki in range(NA_UR)]
            q_rows.append(jnp.concatenate(blocks, axis=-1))
        tabs.append(jnp.concatenate(q_rows, axis=-2))
    return jnp.stack(tabs, axis=1)


def _na_kernel(q_ref, k_ref, v_ref, km_ref, vm_ref, bias_ref, o_ref):
    g = pl.program_id(0)
    us = jnp.minimum(jnp.clip(NA_QR * g - NA_KH // 2, 0, ROWS - NA_KH), ROWS - NA_UR)
    start = pl.multiple_of(us * GRID_W, GRID_W)
    for hp in range(NA_HEADS // 2):
        sl = slice(hp * 128, (hp + 1) * 128)
        qq = _split_halves(q_ref[:, sl])
        kw = k_ref[pl.ds(start, NA_UN), sl]
        vw = v_ref[pl.ds(start, NA_UN), sl]
        s = lax.dot_general(qq, kw, NT_DIMS, preferred_element_type=F32)
        s = s + jnp.concatenate([bias_ref[2 * hp], bias_ref[2 * hp + 1]], axis=0)
        sm = lax.dot_general(qq, km_ref[:, sl], NT_DIMS, preferred_element_type=F32)
        m = jnp.maximum(jnp.max(s, axis=-1, keepdims=True), jnp.max(sm, axis=-1, keepdims=True))
        p = jnp.exp(s - m)
        pm = jnp.exp(sm - m)
        inv = 1.0 / (jnp.sum(p, axis=-1, keepdims=True) + jnp.sum(pm, axis=-1, keepdims=True))
        o2 = (jnp.dot((p * inv).astype(BF16), vw, preferred_element_type=F32)
              + jnp.dot((pm * inv).astype(BF16), vm_ref[:, sl], preferred_element_type=F32))
        lane = lax.broadcasted_iota(jnp.int32, (NA_QN, 128), 1)
        o_ref[:, sl] = jnp.where(lane < 64, o2[:NA_QN], o2[NA_QN:]).astype(BF16)


def _na_attention(qna, kna, vna, bias, layer, nb, T):
    R = nb * SEQ
    meta_blk0 = R // N_META
    last = NA_GROUPS - 1

    def tid(g, b):
        return (layer, jnp.where(g == 0, 0, jnp.where(g == last, 2, 1)), 0, 0, 0)

    return pl.pallas_call(
        _na_kernel,
        grid=(NA_GROUPS, nb),
        in_specs=[
            pl.BlockSpec((NA_QN, NA_COLS), lambda g, b: (b * NA_GROUPS + g, 0)),
            pl.BlockSpec((SEQ, NA_COLS), lambda g, b: (b, 0)),
            pl.BlockSpec((SEQ, NA_COLS), lambda g, b: (b, 0)),
            pl.BlockSpec((N_META, NA_COLS), lambda g, b: (meta_blk0 + b, 0)),
            pl.BlockSpec((N_META, NA_COLS), lambda g, b: (meta_blk0 + b, 0)),
            pl.BlockSpec((None, None, NA_HEADS, NA_QN, NA_UN), tid),
        ],
        out_specs=pl.BlockSpec((NA_QN, NA_COLS), lambda g, b: (b * NA_GROUPS + g, 0)),
        out_shape=jax.ShapeDtypeStruct((T, NA_COLS), BF16),
        compiler_params=_cparams(("parallel", "parallel")),
        name="nbhd_attn",
    )(qna, kna, vna, kna, vna, bias)


def _meta_kernel(qd_ref, kr_ref, vr_ref, kdm_ref, vdm_ref, qn_ref, knm_ref, vnm_ref,
                 lq1_ref, lk1_ref, lq2_ref, lk2_ref, sg_ref, oda_in, ona_in, oda_ref, ona_ref,
                 *, lam_init, nb):
    del oda_in, ona_in
    i = pl.program_id(0)

    @pl.when(i < nb)
    def _():
        lam = _lambda(lq1_ref, lk1_ref, lq2_ref, lk2_ref, lam_init)
        _diff_attend(qd_ref[...], kr_ref, vr_ref, kdm_ref, vdm_ref, lam, sg_ref[...], lam_init, oda_ref)
        for hp in range(NA_HEADS // 2):
            sl = slice(hp * 128, (hp + 1) * 128)
            qq = _split_halves(qn_ref[:, sl])
            s = lax.dot_general(qq, knm_ref[:, sl], NT_DIMS, preferred_element_type=F32)
            p = jnp.exp(s - jnp.max(s, axis=-1, keepdims=True))
            p = p * (1.0 / jnp.sum(p, axis=-1, keepdims=True))
            o2 = jnp.dot(p.astype(BF16), vnm_ref[:, sl], preferred_element_type=F32)
            lane = lax.broadcasted_iota(jnp.int32, (N_META, 128), 1)
            ona_ref[:, sl] = jnp.where(lane < 64, o2[:N_META], o2[N_META:]).astype(BF16)

    @pl.when(i >= nb)
    def _():
        oda_ref[...] = jnp.zeros_like(oda_ref)
        ona_ref[...] = jnp.zeros_like(ona_ref)


def _meta_queries(qda, kda, vda, qna, kna, vna, lams, subln, o_da, o_na, layer, lam_init, nb, T):
    R = nb * SEQ
    meta_blk0 = R // N_META
    n_steps = (T - R) // N_META

    def seq(i):
        return (jnp.minimum(i, nb - 1), 0)

    def meta_in(i):
        return (meta_blk0 + jnp.minimum(i, nb - 1), 0)

    def meta_out(i):
        return (meta_blk0 + i, 0)

    mspec = pl.BlockSpec((N_META, 512), meta_in)
    return pl.pallas_call(
        functools.partial(_meta_kernel, lam_init=lam_init, nb=nb),
        grid=(n_steps,),
        in_specs=[mspec, pl.BlockSpec((SEQ, 512), seq), pl.BlockSpec((SEQ, 512), seq), mspec, mspec,
                  mspec, mspec, mspec] + _lam_specs(layer) +
                 [pl.BlockSpec(memory_space=pl.ANY), pl.BlockSpec(memory_space=pl.ANY)],
        out_specs=[pl.BlockSpec((N_META, 512), meta_out), pl.BlockSpec((N_META, 512), meta_out)],
        out_shape=[jax.ShapeDtypeStruct((T, 512), BF16), jax.ShapeDtypeStruct((T, 512), BF16)],
        input_output_aliases={13: 0, 14: 1},
        compiler_params=_cparams(("arbitrary",)),
        name="meta_queries",
    )(qda, kda, vda, kda, vda, qna, kna, vna, *lams, subln, o_da, o_na)


def _merge_kernel(oda_ref, ona_ref, sga_ref, sgb_ref, h_ref, wa_ref, wb_ref, wo_ref, out_ref):
    ya = jnp.dot(oda_ref[...], wa_ref[...], preferred_element_type=F32)
    yb = jnp.dot(ona_ref[...], wb_ref[...], preferred_element_type=F32)
    merged = sga_ref[...].astype(F32) * ya + sgb_ref[...].astype(F32) * yb
    out_ref[...] = h_ref[...] + jnp.dot(merged.astype(BF16), wo_ref[...], preferred_element_type=F32)


def _merge(o_da, o_na, sga, sgb, h, wa, wb, wo, layer):
    T = h.shape[0]

    def row(i):
        return (i, 0)

    def lyr(i):
        return (layer, 0, 0)

    return pl.pallas_call(
        _merge_kernel,
        grid=(T // TOK,),
        in_specs=[
            pl.BlockSpec((TOK, 512), row), pl.BlockSpec((TOK, 512), row),
            pl.BlockSpec((TOK, D_MODEL), row), pl.BlockSpec((TOK, D_MODEL), row),
            pl.BlockSpec((TOK, D_MODEL), row),
            pl.BlockSpec((None, 512, D_MODEL), lyr), pl.BlockSpec((None, 512, D_MODEL), lyr),
            pl.BlockSpec((None, D_MODEL, D_MODEL), lyr),
        ],
        out_specs=pl.BlockSpec((TOK, D_MODEL), row),
        out_shape=jax.ShapeDtypeStruct((T, D_MODEL), F32),
        input_output_aliases={4: 0},
        compiler_params=_cparams(("parallel",)),
        name="merge",
    )(o_da, o_na, sga, sgb, h, wa, wb, wo)


def _gate_chunk(i0, tiles, hid_ref, wT_ref, r2_ref, p2_ref, cnti_ref, w1_ref):
    zero = jnp.zeros((), BF16)
    for t in tiles:
        ls = slice(t * LANES, (t + 1) * LANES)
        for kb in range(PEER_NKEYS // GATE_KEYS):
            ks = slice(kb * GATE_KEYS, (kb + 1) * GATE_KEYS)
            for ig in range(PEER_IPC // GATE_IG):
                acc = [None] * GATE_IG
                for h in range(PEER_HEADS):
                    r2t = r2_ref[h, ks, ls]
                    p2t = p2_ref[h, ks, ls]
                    cnt8 = cnti_ref[h, pl.ds(i0, PEER_IPC), ls]
                    w8 = w1_ref[h, pl.ds(i0, PEER_IPC), ls]
                    for q in range(GATE_IG):
                        ii = ig * GATE_IG + q
                        cb = cnt8[ii:ii + 1].astype(BF16)
                        wb = w8[ii:ii + 1].astype(BF16)
                        term = jnp.where(r2t < cb, p2t * wb, zero)
                        acc[q] = term if acc[q] is None else acc[q] + term
                for q in range(GATE_IG):
                    ii = ig * GATE_IG + q
                    rows = slice(ii * PEER_NKEYS + kb * GATE_KEYS, ii * PEER_NKEYS + (kb + 1) * GATE_KEYS)
                    hid = hid_ref[rows, ls]
                    act = 0.5 * hid * (1.0 + lax.erf(hid * math.sqrt(0.5)))
                    wT_ref[rows, ls] = acc[q] * act.astype(BF16)


def _peer_kernel(h_ref, g_ref, wqT_ref, keys_ref, u_ref, vTa_ref, vTb_ref, vTl_ref, out_ref,
                 cT_ref, qT_ref, s_ref, rk_ref, r2_ref, w1_ref, p2_ref, cnti_ref,
                 s1s_ref, s2s_ref, cnt_ref, zinv_ref, hida_ref, hidb_ref, wTa_ref, wTb_ref, acc_ref):
    j = pl.program_id(1)
    ntile = TOK // LANES

    def head_rows(h):
        return slice(h * PEER_NKEYS, (h + 1) * PEER_NKEYS)

    @pl.when(j == 0)
    def _route():
        wTb_ref[...] = jnp.zeros_like(wTb_ref)
        acc_ref[...] = jnp.zeros_like(acc_ref)
        c = _rms(h_ref[...], g_ref[...])
        cT_ref[...] = c.T.astype(BF16)
        qT_ref[...] = jnp.dot(wqT_ref[...], cT_ref[...], preferred_element_type=F32).astype(BF16)
        kiota = lax.broadcasted_iota(jnp.int32, (PEER_NKEYS, TOK), 0).astype(F32)

        for h in range(PEER_HEADS):
            for part in range(2):
                rows = slice((2 * h + part) * 128, (2 * h + part + 1) * 128)
                s = jnp.dot(keys_ref[h, part], qT_ref[rows, :], preferred_element_type=F32)
                s_ref[...] = s
                rk_ref[...] = jnp.full((PEER_NKEYS, TOK), float(PEER_TOPK), F32)
                sorted_ref = s1s_ref if part == 0 else s2s_ref

                def extract(a, carry):
                    sv = s_ref[...]
                    m = jnp.max(sv, axis=0, keepdims=True)
                    first = jnp.min(jnp.where(sv == m, kiota, float(PEER_NKEYS)), axis=0, keepdims=True)
                    hit = kiota == first
                    s_ref[...] = jnp.where(hit, -jnp.inf, sv)
                    rk_ref[...] = jnp.where(hit, a.astype(F32), rk_ref[...])
                    sorted_ref[a, pl.ds(h, 1), :] = m
                    return carry

                lax.fori_loop(0, PEER_TOPK, extract, 0)
                e = jnp.exp(s - sorted_ref[0, pl.ds(h, 1), :])
                if part == 0:
                    hida_ref[head_rows(h), :] = rk_ref[...]
                    w1_ref[h] = e
                else:
                    r2_ref[h] = rk_ref[...].astype(BF16)
                    p2_ref[h] = e.astype(BF16)

        def joint(t, carry):
            ls = pl.ds(pl.multiple_of(t * LANES, LANES), LANES)
            s1 = [s1s_ref[a, :, ls] for a in range(PEER_TOPK)]
            s2 = [s2s_ref[b, :, ls] for b in range(PEER_TOPK)]
            cs = [s1[a] + s2[b] for (a, b) in PEER_CAND]
            ahead = [jnp.zeros((PEER_HEADS, LANES), F32) for _ in range(NCAND)]
            behind = [jnp.zeros((PEER_HEADS, LANES), F32) for _ in range(NCAND)]
            for c in range(NCAND):
                for c2 in range(c):
                    w = jnp.where(cs[c2] >= cs[c], 1.0, 0.0)
                    ahead[c] = ahead[c] + w
                    behind[c2] = behind[c2] + w
            e1 = [jnp.exp(s1[a] - s1[0]) for a in range(PEER_TOPK)]
            e2 = [jnp.exp(s2[b] - s2[0]) for b in range(PEER_TOPK)]
            z = jnp.zeros((PEER_HEADS, LANES), F32)
            cnt = [jnp.zeros((PEER_HEADS, LANES), F32) for _ in range(PEER_TOPK)]
            for c, (a, b) in enumerate(PEER_CAND):
                rank = ahead[c] + (float(NCAND - 1 - c) - behind[c])
                sel = jnp.where(rank < float(PEER_TOPK), 1.0, 0.0)
                cnt[a] = cnt[a] + sel
                z = z + sel * (e1[a] * e2[b])
            for a in range(PEER_TOPK):
                cnt_ref[a, :, ls] = cnt[a]
            zinv_ref[:, ls] = 1.0 / z
            return carry

        lax.fori_loop(0, ntile, joint, 0)

        for h in range(PEER_HEADS):
            r1 = hida_ref[head_rows(h), :]
            ci = jnp.zeros((PEER_NKEYS, TOK), F32)
            for a in range(PEER_TOPK):
                ci = jnp.where(r1 == float(a), cnt_ref[a, pl.ds(h, 1), :], ci)
            cnti_ref[h] = ci
            w1_ref[h] = w1_ref[h] * zinv_ref[pl.ds(h, 1), :]

    half = TOK // 2
    halves = (slice(0, half), slice(half, TOK))
    half_tiles = (tuple(range(0, ntile // 2)), tuple(range(ntile // 2, ntile)))
    ia = pl.multiple_of(j * (2 * PEER_IPC), PEER_IPC)
    ib = pl.multiple_of(j * (2 * PEER_IPC) + PEER_IPC, PEER_IPC)

    def mm_hid(hid_ref, urows, hf):
        hid_ref[:, halves[hf]] = jnp.dot(u_ref[urows, :], cT_ref[:, halves[hf]], preferred_element_type=F32)

    def mm_out(vT_ref, wT_ref, hf):
        acc_ref[:, halves[hf]] += jnp.dot(vT_ref[...], wT_ref[:, halves[hf]], preferred_element_type=F32)

    def gate(i0, hid_ref, wT_ref, hf):
        _gate_chunk(i0, half_tiles[hf], hid_ref, wT_ref, r2_ref, p2_ref, cnti_ref, w1_ref)

    rows_a, rows_b = slice(0, PEER_EC), slice(PEER_EC, 2 * PEER_EC)
    mm_hid(hida_ref, rows_a, 0)
    mm_hid(hida_ref, rows_a, 1)
    mm_out(vTb_ref, wTb_ref, 0)
    gate(ia, hida_ref, wTa_ref, 0)
    mm_out(vTb_ref, wTb_ref, 1)
    mm_hid(hidb_ref, rows_b, 0)
    gate(ia, hida_ref, wTa_ref, 1)
    mm_out(vTa_ref, wTa_ref, 0)
    mm_hid(hidb_ref, rows_b, 1)
    gate(ib, hidb_ref, wTb_ref, 0)
    mm_out(vTa_ref, wTa_ref, 1)
    gate(ib, hidb_ref, wTb_ref, 1)

    @pl.when(j == PEER_NSTEP - 1)
    def _():
        last = jnp.dot(vTl_ref[...], wTb_ref[...], preferred_element_type=F32)
        out_ref[...] = h_ref[...] + (acc_ref[...] + last).T


def _peer(h, gains, wqT, keys, u, vT, layer):
    T = h.shape[0]
    hk = (PEER_HEADS, PEER_NKEYS, TOK)
    once = pl.Buffered(1)
    return pl.pallas_call(
        _peer_kernel,
        grid=(T // TOK, PEER_NSTEP),
        in_specs=[
            pl.BlockSpec((TOK, D_MODEL), lambda t, j: (t, 0)),
            pl.BlockSpec((None, 1, D_MODEL), lambda t, j: (layer, 0, 0)),
            pl.BlockSpec((None, PEER_HEADS * 256, D_MODEL), lambda t, j: (layer, 0, 0), pipeline_mode=once),
            pl.BlockSpec((None, PEER_HEADS, 2, PEER_NKEYS, 128), lambda t, j: (layer, 0, 0, 0, 0),
                         pipeline_mode=once),
            pl.BlockSpec((None, 2 * PEER_EC, D_MODEL), lambda t, j: (layer, j, 0)),
            pl.BlockSpec((None, D_MODEL, PEER_EC), lambda t, j: (layer, 0, 2 * j)),
            pl.BlockSpec((None, D_MODEL, PEER_EC), lambda t, j: (layer, 0, jnp.maximum(2 * j - 1, 0))),
            pl.BlockSpec((None, D_MODEL, PEER_EC), lambda t, j: (layer, 0, PEER_NCH - 1), pipeline_mode=once),
        ],
        out_specs=pl.BlockSpec((TOK, D_MODEL), lambda t, j: (t, 0)),
        out_shape=jax.ShapeDtypeStruct((T, D_MODEL), F32),
        scratch_shapes=[
            pltpu.VMEM((D_MODEL, TOK), BF16),
            pltpu.VMEM((PEER_HEADS * 256, TOK), BF16),
            pltpu.VMEM((PEER_NKEYS, TOK), F32),
            pltpu.VMEM((PEER_NKEYS, TOK), F32),
            pltpu.VMEM(hk, BF16),
            pltpu.VMEM(hk, F32),
            pltpu.VMEM(hk, BF16),
            pltpu.VMEM(hk, F32),
            pltpu.VMEM((PEER_TOPK, PEER_HEADS, TOK), F32),
            pltpu.VMEM((PEER_TOPK, PEER_HEADS, TOK), F32),
            pltpu.VMEM((PEER_TOPK, PEER_HEADS, TOK), F32),
            pltpu.VMEM((PEER_HEADS, TOK), F32),
            pltpu.VMEM((PEER_EC, TOK), F32),
            pltpu.VMEM((PEER_EC, TOK), F32),
            pltpu.VMEM((PEER_EC, TOK), BF16),
            pltpu.VMEM((PEER_EC, TOK), BF16),
            pltpu.VMEM((D_MODEL, TOK), F32),
        ],
        input_output_aliases={0: 0},
        compiler_params=_cparams(("parallel", "arbitrary")),
        name="peer",
    )(h, gains, wqT, keys, u, vT, vT, vT)


def _final_kernel(h_ref, g_ref, o_ref):
    o_ref[...] = _rms(h_ref[...], g_ref[...])


def _final_norm(h, gain, blk0, nblk):
    return pl.pallas_call(
        _final_kernel,
        grid=(nblk,),
        in_specs=[pl.BlockSpec((TOK, D_MODEL), lambda i: (blk0 + i, 0)),
                  pl.BlockSpec((1, D_MODEL), lambda i: (0, 0))],
        out_specs=pl.BlockSpec((TOK, D_MODEL), lambda i: (i, 0)),
        out_shape=jax.ShapeDtypeStruct((nblk * TOK, D_MODEL), F32),
        compiler_params=_cparams(("parallel",)),
        name="final_norm",
    )(h, gain)


def _rope_tables():
    half = DA_HEAD // 2
    inv = 1.0 / (ROPE_THETA ** (jnp.arange(half, dtype=F32) * 2.0 / DA_HEAD))
    pos = jnp.concatenate([jnp.arange(N_META, N_META + SEQ, dtype=F32),
                           jnp.tile(jnp.arange(N_META, dtype=F32), TOK // N_META)])
    ang = pos[:, None] * inv[None, :]
    cos, sin = jnp.cos(ang), jnp.sin(ang)
    cos_t = jnp.tile(cos, (1, 512 // half))
    sin_t = jnp.tile(jnp.concatenate([-sin, sin], axis=1), (1, 512 // DA_HEAD))
    return cos_t, sin_t


def kernel(x_prompt, x_sample, meta_tokens, norm_mix, w_in, lambda_q1, lambda_k1, lambda_q2, lambda_k2,
           subln_gain, na_rpb, w_branch_a, w_branch_b, w_out, norm_ffn, peer_wq, peer_keys, peer_u,
           peer_v, norm_final):
    depth = w_in.shape[0]
    nb1, nb2 = x_prompt.shape[0], x_sample.shape[0]
    nb = nb1 + nb2
    R = nb * SEQ
    T = R + _round_up(nb * N_META, TOK)

    h = jnp.concatenate([
        x_prompt.reshape(nb1 * SEQ, D_MODEL), x_sample.reshape(nb2 * SEQ, D_MODEL),
        jnp.broadcast_to(meta_tokens[None], (nb, N_META, D_MODEL)).reshape(nb * N_META, D_MODEL),
        jnp.zeros((T - R - nb * N_META, D_MODEL), F32)], axis=0)

    swap = np.array([(c // DA_HEAD) * DA_HEAD + (c % DA_HEAD + DA_HEAD // 2) % DA_HEAD for c in range(512)])
    wq, wk = w_in[:, :, :512], w_in[:, :, 512:1024]
    w_ext = jnp.concatenate([wq, wq[:, :, swap], wk, wk[:, :, swap], w_in[:, :, 1024:]], axis=2).astype(BF16)
    cos_t, sin_t = _rope_tables()
    bias = _na_bias_tables(na_rpb)
    wa, wb, wo = w_branch_a.astype(BF16), w_branch_b.astype(BF16), w_out.astype(BF16)
    wqT = jnp.transpose(peer_wq, (0, 2, 1)).astype(BF16)
    keys = peer_keys.astype(BF16)
    u = peer_u.astype(BF16)
    vT = jnp.transpose(peer_v, (0, 2, 1)).astype(BF16)
    lams = [x.reshape(depth, 1, DA_HEAD) for x in (lambda_q1, lambda_k1, lambda_q2, lambda_k2)]
    subln = subln_gain.reshape(depth, 1, 2 * DA_HEAD)
    g_mix = norm_mix.reshape(depth, 1, D_MODEL)
    g_ffn = norm_ffn.reshape(depth, 1, D_MODEL)

    for l in range(depth):
        lam_init = 0.8 - 0.6 * math.exp(-0.3 * l)
        qda, kda, vda, qna, kna, vna, sga, sgb = _in_proj(h, g_mix, w_ext, cos_t, sin_t, l, R // TOK)
        o_da = _diff_attention(qda, kda, vda, lams, subln, l, lam_init, nb, T)
        o_na = _na_attention(qna, kna, vna, bias, l, nb, T)
        o_da, o_na = _meta_queries(qda, kda, vda, qna, kna, vna, lams, subln, o_da, o_na, l, lam_init, nb, T)
        h = _merge(o_da, o_na, sga, sgb, h, wa, wb, wo, l)
        h = _peer(h, g_ffn, wqT, keys, u, vT, l)

    gf = norm_final.reshape(1, D_MODEL)
    y1 = _final_norm(h, gf, 0, nb1 * SEQ // TOK).reshape(nb1, SEQ, D_MODEL)
    y2 = _final_norm(h, gf, nb1 * SEQ // TOK, nb2 * SEQ // TOK).reshape(nb2, SEQ, D_MODEL)
    return (y1, y2)
```

```python
import functools
import math

import numpy as np
import jax
import jax.numpy as jnp
from jax import lax
from jax.experimental import pallas as pl
from jax.experimental.pallas import tpu as pltpu

F32 = jnp.float32
BF16 = jnp.bfloat16

D_MODEL = 1024
SEQ = 2048
N_META = 16
GRID_W = 64
ROWS = SEQ // GRID_W
ROPE_THETA = 10000.0
RMS_EPS = 1e-6
DA_HEADS = 4
DA_HEAD = 64
NA_HEADS = 8
NA_HEAD = 64
NA_KH = 8
NA_KW = 16
NA_COLS = 512
PEER_HEADS = 8
PEER_NKEYS = 128
PEER_TOPK = 16
PEER_EXPERTS = PEER_NKEYS * PEER_NKEYS

LANES = 128
TOK = 512
DA_QB = 256
NA_QR = 4
NA_GROUPS = ROWS // NA_QR
NA_UR = NA_QR + NA_KH - 1
NA_QN = NA_QR * GRID_W
NA_UN = NA_UR * GRID_W
PEER_EC = 1024
PEER_NCH = PEER_EXPERTS // PEER_EC
PEER_NSTEP = PEER_NCH // 2
PEER_IPC = PEER_EC // PEER_NKEYS
GATE_KEYS = 128
GATE_IG = 2
VMEM_LIMIT = 60 * 1024 * 1024

PEER_CAND = tuple((a, b) for a in range(PEER_TOPK) for b in range(PEER_TOPK)
                  if (a + 1) * (b + 1) <= PEER_TOPK)
NCAND = len(PEER_CAND)

NT_DIMS = (((1,), (1,)), ((), ()))


def _round_up(x, m):
    return (x + m - 1) // m * m


def _cparams(sem):
    return pltpu.CompilerParams(dimension_semantics=sem, vmem_limit_bytes=VMEM_LIMIT)


def _rms(x, gain):
    return x * lax.rsqrt(jnp.mean(x * x, axis=-1, keepdims=True) + RMS_EPS) * gain


def _in_proj_kernel(h_ref, g_ref, w_ref, cos_ref, sin_ref,
                    qda_ref, kda_ref, vda_ref, qna_ref, kna_ref, vna_ref, sga_ref, sgb_ref, xn_ref):
    xn_ref[...] = _rms(h_ref[...], g_ref[...]).astype(BF16)

    def mm(c0, n):
        return jnp.dot(xn_ref[...], w_ref[:, c0:c0 + n], preferred_element_type=F32)

    cos = cos_ref[...]
    sin = sin_ref[...]
    qda_ref[...] = ((mm(0, 512) * cos + mm(512, 512) * sin) * (DA_HEAD ** -0.5)).astype(BF16)
    kda_ref[...] = (mm(1024, 512) * cos + mm(1536, 512) * sin).astype(BF16)
    vda_ref[...] = mm(2048, 512).astype(BF16)
    qna_ref[...] = (mm(2560, 512) * (NA_HEAD ** -0.5)).astype(BF16)
    kna_ref[...] = mm(3072, 512).astype(BF16)
    vna_ref[...] = mm(3584, 512).astype(BF16)
    sga_ref[...] = jax.nn.sigmoid(mm(4096, 1024)).astype(BF16)
    sgb_ref[...] = jax.nn.sigmoid(mm(5120, 1024)).astype(BF16)


def _in_proj(h, gains, w_ext, cos_t, sin_t, layer, n_real_blocks):
    T = h.shape[0]
    per_seq = SEQ // TOK

    def row(i):
        return (i, 0)

    def tbl(i):
        return (jnp.where(i < n_real_blocks, i % per_seq, per_seq), 0)

    outs = [jax.ShapeDtypeStruct((T, 512), BF16)] * 6 + [jax.ShapeDtypeStruct((T, D_MODEL), BF16)] * 2
    return pl.pallas_call(
        _in_proj_kernel,
        grid=(T // TOK,),
        in_specs=[
            pl.BlockSpec((TOK, D_MODEL), row),
            pl.BlockSpec((None, 1, D_MODEL), lambda i: (layer, 0, 0)),
            pl.BlockSpec((None, D_MODEL, 6144), lambda i: (layer, 0, 0)),
            pl.BlockSpec((TOK, 512), tbl),
            pl.BlockSpec((TOK, 512), tbl),
        ],
        out_specs=[pl.BlockSpec((TOK, 512), row)] * 6 + [pl.BlockSpec((TOK, D_MODEL), row)] * 2,
        out_shape=outs,
        scratch_shapes=[pltpu.VMEM((TOK, D_MODEL), BF16)],
        compiler_params=_cparams(("parallel",)),
        name="in_proj",
    )(h, gains, w_ext, cos_t, sin_t)


def _lambda(lq1_ref, lk1_ref, lq2_ref, lk2_ref, lam_init):
    a = jnp.sum(lq1_ref[...] * lk1_ref[...], axis=-1, keepdims=True)
    b = jnp.sum(lq2_ref[...] * lk2_ref[...], axis=-1, keepdims=True)
    return jnp.exp(a) - jnp.exp(b) + lam_init


def _split_halves(x):
    lane = lax.broadcasted_iota(jnp.int32, x.shape, 1)
    zero = jnp.zeros_like(x)
    return jnp.concatenate([jnp.where(lane < 64, x, zero), jnp.where(lane >= 64, x, zero)], axis=0)


def _diff_attend(q, kr_ref, vr_ref, km_ref, vm_ref, lam, gain, lam_init, o_ref):
    n = q.shape[0]
    for h in range(DA_HEADS):
        sl = slice(h * 128, (h + 1) * 128)
        qq = _split_halves(q[:, sl])
        s = lax.dot_general(qq, kr_ref[:, sl], NT_DIMS, preferred_element_type=F32)
        sm = lax.dot_general(qq, km_ref[:, sl], NT_DIMS, preferred_element_type=F32)
        m = jnp.maximum(jnp.max(s, axis=-1, keepdims=True), jnp.max(sm, axis=-1, keepdims=True))
        p = jnp.exp(s - m)
        pm = jnp.exp(sm - m)
        inv = 1.0 / (jnp.sum(p, axis=-1, keepdims=True) + jnp.sum(pm, axis=-1, keepdims=True))
        c1 = inv[:n]
        c2 = inv[n:] * lam
        a = (p[:n] * c1 - p[n:] * c2).astype(BF16)
        am = (pm[:n] * c1 - pm[n:] * c2).astype(BF16)
        o = (jnp.dot(a, vr_ref[:, sl], preferred_element_type=F32)
             + jnp.dot(am, vm_ref[:, sl], preferred_element_type=F32))
        o = _rms(o, gain) * (1.0 - lam_init)
        o_ref[:, sl] = o.astype(BF16)


def _da_kernel(q_ref, kr_ref, vr_ref, km_ref, vm_ref, lq1_ref, lk1_ref, lq2_ref, lk2_ref, sg_ref,
               o_ref, *, lam_init):
    lam = _lambda(lq1_ref, lk1_ref, lq2_ref, lk2_ref, lam_init)
    _diff_attend(q_ref[...], kr_ref, vr_ref, km_ref, vm_ref, lam, sg_ref[...], lam_init, o_ref)


def _lam_specs(layer):
    return [pl.BlockSpec((None, 1, DA_HEAD), lambda *_: (layer, 0, 0))] * 4 + \
           [pl.BlockSpec((None, 1, 2 * DA_HEAD), lambda *_: (layer, 0, 0))]


def _diff_attention(qda, kda, vda, lams, subln, layer, lam_init, nb, T):
    R = nb * SEQ
    qpb = SEQ // DA_QB
    meta_blk0 = R // N_META
    return pl.pallas_call(
        functools.partial(_da_kernel, lam_init=lam_init),
        grid=(nb, qpb),
        in_specs=[
            pl.BlockSpec((DA_QB, 512), lambda b, j: (b * qpb + j, 0)),
            pl.BlockSpec((SEQ, 512), lambda b, j: (b, 0)),
            pl.BlockSpec((SEQ, 512), lambda b, j: (b, 0)),
            pl.BlockSpec((N_META, 512), lambda b, j: (meta_blk0 + b, 0)),
            pl.BlockSpec((N_META, 512), lambda b, j: (meta_blk0 + b, 0)),
        ] + _lam_specs(layer),
        out_specs=pl.BlockSpec((DA_QB, 512), lambda b, j: (b * qpb + j, 0)),
        out_shape=jax.ShapeDtypeStruct((T, 512), BF16),
        compiler_params=_cparams(("parallel", "parallel")),
        name="diff_attn",
    )(qda, kda, vda, kda, vda, *lams, subln)


def _na_union_start(g):
    return min(max(NA_QR * g - NA_KH // 2, 0), ROWS - NA_KH, ROWS - NA_UR)


def _na_table_id(g):
    return 0 if g == 0 else (2 if g == NA_GROUPS - 1 else 1)


def _na_bias_tables(rpb):
    depth = rpb.shape[0]
    qc = np.arange(GRID_W)
    kc = np.arange(GRID_W)
    dc = np.clip(kc[None, :] - qc[:, None], -(NA_KW - 1), NA_KW - 1) + (NA_KW - 1)
    onehot = (dc[None] == np.arange(2 * NA_KW - 1)[:, None, None]).astype(np.float32)
    cs = np.clip(qc - NA_KW // 2, 0, GRID_W - NA_KW)
    col_ok = (kc[None, :] >= cs[:, None]) & (kc[None, :] < cs[:, None] + NA_KW)
    t = jnp.einsum("lhrc,cqk->lhrqk", rpb.astype(F32), jnp.asarray(onehot), precision=lax.Precision.HIGHEST)
    t = jnp.where(jnp.asarray(col_ok), t, -jnp.inf)
    neg = jnp.full((depth, NA_HEADS, GRID_W, GRID_W), -jnp.inf, F32)
    tabs = []
    for g in (0, 1, NA_GROUPS - 1):
        us = _na_union_start(g)
        q_rows = []
        for ri in range(NA_QR):
            r = NA_QR * g + ri
            rs = min(max(r - NA_KH // 2, 0), ROWS - NA_KH)
            blocks = [t[:, :, us + ki - r + NA_KH - 1] if rs <= us + ki < rs + NA_KH else neg
                      for ki in range(NA_UR)]
            q_rows.append(jnp.concatenate(blocks, axis=-1))
        tabs.append(jnp.concatenate(q_rows, axis=-2))
    return jnp.stack(tabs, axis=1)


def _na_kernel(q_ref, k_ref, v_ref, km_ref, vm_ref, bias_ref, o_ref):
    g = pl.program_id(0)
    us = jnp.minimum(jnp.clip(NA_QR * g - NA_KH // 2, 0, ROWS - NA_KH), ROWS - NA_UR)
    start = pl.multiple_of(us * GRID_W, GRID_W)
    for hp in range(NA_HEADS // 2):
        sl = slice(hp * 128, (hp + 1) * 128)
        qq = _split_halves(q_ref[:, sl])
        kw = k_ref[pl.ds(start, NA_UN), sl]
        vw = v_ref[pl.ds(start, NA_UN), sl]
        s = lax.dot_general(qq, kw, NT_DIMS, preferred_element_type=F32)
        s = s + jnp.concatenate([bias_ref[2 * hp], bias_ref[2 * hp + 1]], axis=0)
        sm = lax.dot_general(qq, km_ref[:, sl], NT_DIMS, preferred_element_type=F32)
        m = jnp.maximum(jnp.max(s, axis=-1, keepdims=True), jnp.max(sm, axis=-1, keepdims=True))
        p = jnp.exp(s - m)
        pm = jnp.exp(sm - m)
        inv = 1.0 / (jnp.sum(p, axis=-1, keepdims=True) + jnp.sum(pm, axis=-1, keepdims=True))
        o2 = (jnp.dot((p * inv).astype(BF16), vw, preferred_element_type=F32)
              + jnp.dot((pm * inv).astype(BF16), vm_ref[:, sl], preferred_element_type=F32))
        lane = lax.broadcasted_iota(jnp.int32, (NA_QN, 128), 1)
        o_ref[:, sl] = jnp.where(lane < 64, o2[:NA_QN], o2[NA_QN:]).astype(BF16)


def _na_attention(qna, kna, vna, bias, layer, nb, T):
    R = nb * SEQ
    meta_blk0 = R // N_META
    last = NA_GROUPS - 1

    def tid(g, b):
        return (layer, jnp.where(g == 0, 0, jnp.where(g == last, 2, 1)), 0, 0, 0)

    return pl.pallas_call(
        _na_kernel,
        grid=(NA_GROUPS, nb),
        in_specs=[
            pl.BlockSpec((NA_QN, NA_COLS), lambda g, b: (b * NA_GROUPS + g, 0)),
            pl.BlockSpec((SEQ, NA_COLS), lambda g, b: (b, 0)),
            pl.BlockSpec((SEQ, NA_COLS), lambda g, b: (b, 0)),
            pl.BlockSpec((N_META, NA_COLS), lambda g, b: (meta_blk0 + b, 0)),
            pl.BlockSpec((N_META, NA_COLS), lambda g, b: (meta_blk0 + b, 0)),
            pl.BlockSpec((None, None, NA_HEADS, NA_QN, NA_UN), tid),
        ],
        out_specs=pl.BlockSpec((NA_QN, NA_COLS), lambda g, b: (b * NA_GROUPS + g, 0)),
        out_shape=jax.ShapeDtypeStruct((T, NA_COLS), BF16),
        compiler_params=_cparams(("parallel", "parallel")),
        name="nbhd_attn",
    )(qna, kna, vna, kna, vna, bias)


def _meta_kernel(qd_ref, kr_ref, vr_ref, kdm_ref, vdm_ref, qn_ref, knm_ref, vnm_ref,
                 lq1_ref, lk1_ref, lq2_ref, lk2_ref, sg_ref, oda_in, ona_in, oda_ref, ona_ref,
                 *, lam_init, nb):
    del oda_in, ona_in
    i = pl.program_id(0)

    @pl.when(i < nb)
    def _():
        lam = _lambda(lq1_ref, lk1_ref, lq2_ref, lk2_ref, lam_init)
        _diff_attend(qd_ref[...], kr_ref, vr_ref, kdm_ref, vdm_ref, lam, sg_ref[...], lam_init, oda_ref)
        for hp in range(NA_HEADS // 2):
            sl = slice(hp * 128, (hp + 1) * 128)
            qq = _split_halves(qn_ref[:, sl])
            s = lax.dot_general(qq, knm_ref[:, sl], NT_DIMS, preferred_element_type=F32)
            p = jnp.exp(s - jnp.max(s, axis=-1, keepdims=True))
            p = p * (1.0 / jnp.sum(p, axis=-1, keepdims=True))
            o2 = jnp.dot(p.astype(BF16), vnm_ref[:, sl], preferred_element_type=F32)
            lane = lax.broadcasted_iota(jnp.int32, (N_META, 128), 1)
            ona_ref[:, sl] = jnp.where(lane < 64, o2[:N_META], o2[N_META:]).astype(BF16)

    @pl.when(i >= nb)
    def _():
        oda_ref[...] = jnp.zeros_like(oda_ref)
        ona_ref[...] = jnp.zeros_like(ona_ref)


def _meta_queries(qda, kda, vda, qna, kna, vna, lams, subln, o_da, o_na, layer, lam_init, nb, T):
    R = nb * SEQ
    meta_blk0 = R // N_META
    n_steps = (T - R) // N_META

    def seq(i):
        return (jnp.minimum(i, nb - 1), 0)

    def meta_in(i):
        return (meta_blk0 + jnp.minimum(i, nb - 1), 0)

    def meta_out(i):
        return (meta_blk0 + i, 0)

    mspec = pl.BlockSpec((N_META, 512), meta_in)
    return pl.pallas_call(
        functools.partial(_meta_kernel, lam_init=lam_init, nb=nb),
        grid=(n_steps,),
        in_specs=[mspec, pl.BlockSpec((SEQ, 512), seq), pl.BlockSpec((SEQ, 512), seq), mspec, mspec,
                  mspec, mspec, mspec] + _lam_specs(layer) +
                 [pl.BlockSpec(memory_space=pl.ANY), pl.BlockSpec(memory_space=pl.ANY)],
        out_specs=[pl.BlockSpec((N_META, 512), meta_out), pl.BlockSpec((N_META, 512), meta_out)],
        out_shape=[jax.ShapeDtypeStruct((T, 512), BF16), jax.ShapeDtypeStruct((T, 512), BF16)],
        input_output_aliases={13: 0, 14: 1},
        compiler_params=_cparams(("arbitrary",)),
        name="meta_queries",
    )(qda, kda, vda, kda, vda, qna, kna, vna, *lams, subln, o_da, o_na)


def _merge_kernel(oda_ref, ona_ref, sga_ref, sgb_ref, h_ref, wa_ref, wb_ref, wo_ref, out_ref):
    ya = jnp.dot(oda_ref[...], wa_ref[...], preferred_element_type=F32)
    yb = jnp.dot(ona_ref[...], wb_ref[...], preferred_element_type=F32)
    merged = sga_ref[...].astype(F32) * ya + sgb_ref[...].astype(F32) * yb
    out_ref[...] = h_ref[...] + jnp.dot(merged.astype(BF16), wo_ref[...], preferred_element_type=F32)


def _merge(o_da, o_na, sga, sgb, h, wa, wb, wo, layer):
    T = h.shape[0]

    def row(i):
        return (i, 0)

    def lyr(i):
        return (layer, 0, 0)

    return pl.pallas_call(
        _merge_kernel,
        grid=(T // TOK,),
        in_specs=[
            pl.BlockSpec((TOK, 512), row), pl.BlockSpec((TOK, 512), row),
            pl.BlockSpec((TOK, D_MODEL), row), pl.BlockSpec((TOK, D_MODEL), row),
            pl.BlockSpec((TOK, D_MODEL), row),
            pl.BlockSpec((None, 512, D_MODEL), lyr), pl.BlockSpec((None, 512, D_MODEL), lyr),
            pl.BlockSpec((None, D_MODEL, D_MODEL), lyr),
        ],
        out_specs=pl.BlockSpec((TOK, D_MODEL), row),
        out_shape=jax.ShapeDtypeStruct((T, D_MODEL), F32),
        input_output_aliases={4: 0},
        compiler_params=_cparams(("parallel",)),
        name="merge",
    )(o_da, o_na, sga, sgb, h, wa, wb, wo)


def _gate_chunk(i0, tiles, hid_ref, wT_ref, r2_ref, p2_ref, cnti_ref, w1_ref):
    zero = jnp.zeros((), BF16)
    for t in tiles:
        ls = slice(t * LANES, (t + 1) * LANES)
        for kb in range(PEER_NKEYS // GATE_KEYS):
            ks = slice(kb * GATE_KEYS, (kb + 1) * GATE_KEYS)
            for ig in range(PEER_IPC // GATE_IG):
                acc = [None] * GATE_IG
                for h in range(PEER_HEADS):
                    r2t = r2_ref[h, ks, ls]
                    p2t = p2_ref[h, ks, ls]
                    cnt8 = cnti_ref[h, pl.ds(i0, PEER_IPC), ls]
                    w8 = w1_ref[h, pl.ds(i0, PEER_IPC), ls]
                    for q in range(GATE_IG):
                        ii = ig * GATE_IG + q
                        cb = cnt8[ii:ii + 1].astype(BF16)
                        wb = w8[ii:ii + 1].astype(BF16)
                        term = jnp.where(r2t < cb, p2t * wb, zero)
                        acc[q] = term if acc[q] is None else acc[q] + term
                for q in range(GATE_IG):
                    ii = ig * GATE_IG + q
                    rows = slice(ii * PEER_NKEYS + kb * GATE_KEYS, ii * PEER_NKEYS + (kb + 1) * GATE_KEYS)
                    hid = hid_ref[rows, ls]
                    act = 0.5 * hid * (1.0 + lax.erf(hid * math.sqrt(0.5)))
                    wT_ref[rows, ls] = acc[q] * act.astype(BF16)


def _peer_kernel(h_ref, g_ref, wqT_ref, keys_ref, u_ref, vTa_ref, vTb_ref, vTl_ref, out_ref,
                 cT_ref, qT_ref, s_ref, rk_ref, srt_ref, r2_ref, w1_ref, p2_ref, cnti_ref,
                 s1h_ref, s2h_ref, s1s_ref, s2s_ref, cnt_ref, cnth_ref, zinv_ref,
                 hida_ref, hidb_ref, wTa_ref, wTb_ref, acc_ref):
    j = pl.program_id(1)
    ntile = TOK // LANES

    def top_ranks(s):
        kiota = lax.broadcasted_iota(jnp.int32, (PEER_NKEYS, TOK), 0).astype(F32)
        riota = lax.broadcasted_iota(jnp.int32, (PEER_TOPK, TOK), 0)
        unranked = jnp.full((PEER_NKEYS, TOK), float(PEER_TOPK), F32)
        srt0 = jnp.zeros((PEER_TOPK, TOK), F32)

        def peel(exact):
            def body(a, srt):
                sv = s_ref[...]
                m = jnp.max(sv, axis=0, keepdims=True)
                hit = sv == m
                if exact:
                    first = jnp.min(jnp.where(hit, kiota, float(PEER_NKEYS)), axis=0, keepdims=True)
                    hit = kiota == first
                s_ref[...] = jnp.where(hit, -jnp.inf, sv)
                rk_ref[...] = jnp.where(hit, lax.convert_element_type(a, F32), rk_ref[...])
                return jnp.where(riota == a, m, srt)
            return body

        s_ref[...] = s
        rk_ref[...] = unranked
        srt_ref[...] = lax.fori_loop(0, PEER_TOPK, peel(False), srt0)
        ranked = jnp.sum(jnp.where(rk_ref[...] < float(PEER_TOPK), 1.0, 0.0), axis=0, keepdims=True)
        tied = jnp.max(ranked) > float(PEER_TOPK)

        @pl.when(tied)
        def _():
            s_ref[...] = s
            rk_ref[...] = unranked
            srt_ref[...] = lax.fori_loop(0, PEER_TOPK, peel(True), srt0)

        return rk_ref[...], srt_ref[...]

    @pl.when(j == 0)
    def _route():
        wTb_ref[...] = jnp.zeros_like(wTb_ref)
        acc_ref[...] = jnp.zeros_like(acc_ref)
        c = _rms(h_ref[...], g_ref[...])
        cT_ref[...] = c.T.astype(BF16)
        qT_ref[...] = jnp.dot(wqT_ref[...], cT_ref[...], preferred_element_type=F32).astype(BF16)

        def per_head(h, carry):
            rows = pl.ds(pl.multiple_of(h * PEER_NKEYS, PEER_NKEYS), PEER_NKEYS)
            q1 = qT_ref[pl.ds(pl.multiple_of(2 * h * PEER_NKEYS, PEER_NKEYS), PEER_NKEYS), :]
            q2 = qT_ref[pl.ds(pl.multiple_of((2 * h + 1) * PEER_NKEYS, PEER_NKEYS), PEER_NKEYS), :]
            s1 = jnp.dot(keys_ref[h, 0], q1, preferred_element_type=F32)
            rk, srt = top_ranks(s1)
            hida_ref[rows, :] = rk
            w1_ref[h] = jnp.exp(s1 - srt[0:1])
            s1h_ref[h] = srt
            s2 = jnp.dot(keys_ref[h, 1], q2, preferred_element_type=F32)
            rk, srt = top_ranks(s2)
            r2_ref[h] = rk.astype(BF16)
            p2_ref[h] = jnp.exp(s2 - srt[0:1]).astype(BF16)
            s2h_ref[h] = srt
            return carry

        lax.fori_loop(0, PEER_HEADS, per_head, 0)

        for h in range(PEER_HEADS):
            for a in range(PEER_TOPK):
                s1s_ref[a, h:h + 1, :] = s1h_ref[h, a:a + 1, :]
                s2s_ref[a, h:h + 1, :] = s2h_ref[h, a:a + 1, :]

        def joint(t, carry):
            ls = pl.ds(pl.multiple_of(t * LANES, LANES), LANES)
            s1 = [s1s_ref[a, :, ls] for a in range(PEER_TOPK)]
            s2 = [s2s_ref[b, :, ls] for b in range(PEER_TOPK)]
            cs = [s1[a] + s2[b] for (a, b) in PEER_CAND]
            ahead = [jnp.zeros((PEER_HEADS, LANES), F32) for _ in range(NCAND)]
            behind = [jnp.zeros((PEER_HEADS, LANES), F32) for _ in range(NCAND)]
            for c in range(NCAND):
                for c2 in range(c):
                    w = jnp.where(cs[c2] >= cs[c], 1.0, 0.0)
                    ahead[c] = ahead[c] + w
                    behind[c2] = behind[c2] + w
            e1 = [jnp.exp(s1[a] - s1[0]) for a in range(PEER_TOPK)]
            e2 = [jnp.exp(s2[b] - s2[0]) for b in range(PEER_TOPK)]
            z = jnp.zeros((PEER_HEADS, LANES), F32)
            cnt = [jnp.zeros((PEER_HEADS, LANES), F32) for _ in range(PEER_TOPK)]
            for c, (a, b) in enumerate(PEER_CAND):
                rank = ahead[c] + (float(NCAND - 1 - c) - behind[c])
                sel = jnp.where(rank < float(PEER_TOPK), 1.0, 0.0)
                cnt[a] = cnt[a] + sel
                z = z + sel * (e1[a] * e2[b])
            for a in range(PEER_TOPK):
                cnt_ref[a, :, ls] = cnt[a]
            zinv_ref[:, ls] = 1.0 / z
            return carry

        lax.fori_loop(0, ntile, joint, 0)

        for h in range(PEER_HEADS):
            for a in range(PEER_TOPK):
                cnth_ref[h, a:a + 1, :] = cnt_ref[a, h:h + 1, :]
            cnth_ref[h, PEER_TOPK:PEER_TOPK + 1, :] = zinv_ref[h:h + 1, :]

        def per_head_counts(h, carry):
            rows = pl.ds(pl.multiple_of(h * PEER_NKEYS, PEER_NKEYS), PEER_NKEYS)
            r1 = hida_ref[rows, :]
            cn = cnth_ref[h, 0:PEER_TOPK, :]
            ci = jnp.zeros((PEER_NKEYS, TOK), F32)
            for a in range(PEER_TOPK):
                ci = jnp.where(r1 == float(a), cn[a:a + 1], ci)
            cnti_ref[h] = ci
            w1_ref[h] = w1_ref[h] * cnth_ref[h, PEER_TOPK:PEER_TOPK + 1, :]
            return carry

        lax.fori_loop(0, PEER_HEADS, per_head_counts, 0)

    half = TOK // 2
    halves = (slice(0, half), slice(half, TOK))
    half_tiles = (tuple(range(0, ntile // 2)), tuple(range(ntile // 2, ntile)))
    ia = pl.multiple_of(j * (2 * PEER_IPC), PEER_IPC)
    ib = pl.multiple_of(j * (2 * PEER_IPC) + PEER_IPC, PEER_IPC)

    def mm_hid(hid_ref, urows, hf):
        hid_ref[:, halves[hf]] = jnp.dot(u_ref[urows, :], cT_ref[:, halves[hf]], preferred_element_type=F32)

    def mm_out(vT_ref, wT_ref, hf):
        acc_ref[:, halves[hf]] += jnp.dot(vT_ref[...], wT_ref[:, halves[hf]], preferred_element_type=F32)

    def gate(i0, hid_ref, wT_ref, hf):
        _gate_chunk(i0, half_tiles[hf], hid_ref, wT_ref, r2_ref, p2_ref, cnti_ref, w1_ref)

    rows_a, rows_b = slice(0, PEER_EC), slice(PEER_EC, 2 * PEER_EC)
    mm_hid(hida_ref, rows_a, 0)
    mm_hid(hida_ref, rows_a, 1)
    mm_out(vTb_ref, wTb_ref, 0)
    gate(ia, hida_ref, wTa_ref, 0)
    mm_out(vTb_ref, wTb_ref, 1)
    mm_hid(hidb_ref, rows_b, 0)
    gate(ia, hida_ref, wTa_ref, 1)
    mm_out(vTa_ref, wTa_ref, 0)
    mm_hid(hidb_ref, rows_b, 1)
    gate(ib, hidb_ref, wTb_ref, 0)
    mm_out(vTa_ref, wTa_ref, 1)
    gate(ib, hidb_ref, wTb_ref, 1)

    @pl.when(j == PEER_NSTEP - 1)
    def _():
        last = jnp.dot(vTl_ref[...], wTb_ref[...], preferred_element_type=F32)
        out_ref[...] = h_ref[...] + (acc_ref[...] + last).T


def _peer(h, gains, wqT, keys, u, vT, layer):
    T = h.shape[0]
    hk = (PEER_HEADS, PEER_NKEYS, TOK)
    once = pl.Buffered(1)
    return pl.pallas_call(
        _peer_kernel,
        grid=(T // TOK, PEER_NSTEP),
        in_specs=[
            pl.BlockSpec((TOK, D_MODEL), lambda t, j: (t, 0)),
            pl.BlockSpec((None, 1, D_MODEL), lambda t, j: (layer, 0, 0)),
            pl.BlockSpec((None, PEER_HEADS * 256, D_MODEL), lambda t, j: (layer, 0, 0), pipeline_mode=once),
            pl.BlockSpec((None, PEER_HEADS, 2, PEER_NKEYS, 128), lambda t, j: (layer, 0, 0, 0, 0),
                         pipeline_mode=once),
            pl.BlockSpec((None, 2 * PEER_EC, D_MODEL), lambda t, j: (layer, j, 0)),
            pl.BlockSpec((None, None, D_MODEL, PEER_EC), lambda t, j: (layer, 2 * j, 0, 0)),
            pl.BlockSpec((None, None, D_MODEL, PEER_EC), lambda t, j: (layer, jnp.maximum(2 * j - 1, 0), 0, 0)),
            pl.BlockSpec((None, None, D_MODEL, PEER_EC), lambda t, j: (layer, PEER_NCH - 1, 0, 0),
                         pipeline_mode=once),
        ],
        out_specs=pl.BlockSpec((TOK, D_MODEL), lambda t, j: (t, 0)),
        out_shape=jax.ShapeDtypeStruct((T, D_MODEL), F32),
        scratch_shapes=[
            pltpu.VMEM((D_MODEL, TOK), BF16),
            pltpu.VMEM((PEER_HEADS * 256, TOK), BF16),
            pltpu.VMEM((PEER_NKEYS, TOK), F32),
            pltpu.VMEM((PEER_NKEYS, TOK), F32),
            pltpu.VMEM((PEER_TOPK, TOK), F32),
            pltpu.VMEM(hk, BF16),
            pltpu.VMEM(hk, F32),
            pltpu.VMEM(hk, BF16),
            pltpu.VMEM(hk, F32),
            pltpu.VMEM((PEER_HEADS, PEER_TOPK, TOK), F32),
            pltpu.VMEM((PEER_HEADS, PEER_TOPK, TOK), F32),
            pltpu.VMEM((PEER_TOPK, PEER_HEADS, TOK), F32),
            pltpu.VMEM((PEER_TOPK, PEER_HEADS, TOK), F32),
            pltpu.VMEM((PEER_TOPK, PEER_HEADS, TOK), F32),
            pltpu.VMEM((PEER_HEADS, PEER_TOPK + 8, TOK), F32),
            pltpu.VMEM((PEER_HEADS, TOK), F32),
            pltpu.VMEM((PEER_EC, TOK), F32),
            pltpu.VMEM((PEER_EC, TOK), F32),
            pltpu.VMEM((PEER_EC, TOK), BF16),
            pltpu.VMEM((PEER_EC, TOK), BF16),
            pltpu.VMEM((D_MODEL, TOK), F32),
        ],
        input_output_aliases={0: 0},
        compiler_params=_cparams(("parallel", "arbitrary")),
        name="peer",
    )(h, gains, wqT, keys, u, vT, vT, vT)


def _final_kernel(h_ref, g_ref, o_ref):
    o_ref[...] = _rms(h_ref[...], g_ref[...])


def _final_norm(h, gain, blk0, nblk):
    return pl.pallas_call(
        _final_kernel,
        grid=(nblk,),
        in_specs=[pl.BlockSpec((TOK, D_MODEL), lambda i: (blk0 + i, 0)),
                  pl.BlockSpec((1, D_MODEL), lambda i: (0, 0))],
        out_specs=pl.BlockSpec((TOK, D_MODEL), lambda i: (i, 0)),
        out_shape=jax.ShapeDtypeStruct((nblk * TOK, D_MODEL), F32),
        compiler_params=_cparams(("parallel",)),
        name="final_norm",
    )(h, gain)


def _rope_tables():
    half = DA_HEAD // 2
    inv = 1.0 / (ROPE_THETA ** (jnp.arange(half, dtype=F32) * 2.0 / DA_HEAD))
    pos = jnp.concatenate([jnp.arange(N_META, N_META + SEQ, dtype=F32),
                           jnp.tile(jnp.arange(N_META, dtype=F32), TOK // N_META)])
    ang = pos[:, None] * inv[None, :]
    cos, sin = jnp.cos(ang), jnp.sin(ang)
    cos_t = jnp.tile(cos, (1, 512 // half))
    sin_t = jnp.tile(jnp.concatenate([-sin, sin], axis=1), (1, 512 // DA_HEAD))
    return cos_t, sin_t


def kernel(x_prompt, x_sample, meta_tokens, norm_mix, w_in, lambda_q1, lambda_k1, lambda_q2, lambda_k2,
           subln_gain, na_rpb, w_branch_a, w_branch_b, w_out, norm_ffn, peer_wq, peer_keys, peer_u,
           peer_v, norm_final):
    depth = w_in.shape[0]
    nb1, nb2 = x_prompt.shape[0], x_sample.shape[0]
    nb = nb1 + nb2
    R = nb * SEQ
    T = R + _round_up(nb * N_META, TOK)

    h = jnp.concatenate([
        x_prompt.reshape(nb1 * SEQ, D_MODEL), x_sample.reshape(nb2 * SEQ, D_MODEL),
        jnp.broadcast_to(meta_tokens[None], (nb, N_META, D_MODEL)).reshape(nb * N_META, D_MODEL),
        jnp.zeros((T - R - nb * N_META, D_MODEL), F32)], axis=0)

    swap = np.array([(c // DA_HEAD) * DA_HEAD + (c % DA_HEAD + DA_HEAD // 2) % DA_HEAD for c in range(512)])
    wq, wk = w_in[:, :, :512], w_in[:, :, 512:1024]
    w_ext = jnp.concatenate([wq, wq[:, :, swap], wk, wk[:, :, swap], w_in[:, :, 1024:]], axis=2).astype(BF16)
    cos_t, sin_t = _rope_tables()
    bias = _na_bias_tables(na_rpb)
    wa, wb, wo = w_branch_a.astype(BF16), w_branch_b.astype(BF16), w_out.astype(BF16)
    wqT = jnp.transpose(peer_wq, (0, 2, 1)).astype(BF16)
    keys = peer_keys.astype(BF16)
    u = peer_u.astype(BF16)
    vT = jnp.transpose(peer_v.reshape(depth, PEER_NCH, PEER_EC, D_MODEL), (0, 1, 3, 2)).astype(BF16)
    lams = [x.reshape(depth, 1, DA_HEAD) for x in (lambda_q1, lambda_k1, lambda_q2, lambda_k2)]
    subln = subln_gain.reshape(depth, 1, 2 * DA_HEAD)
    g_mix = norm_mix.reshape(depth, 1, D_MODEL)
    g_ffn = norm_ffn.reshape(depth, 1, D_MODEL)

    for l in range(depth):
        lam_init = 0.8 - 0.6 * math.exp(-0.3 * l)
        qda, kda, vda, qna, kna, vna, sga, sgb = _in_proj(h, g_mix, w_ext, cos_t, sin_t, l, R // TOK)
        o_da = _diff_attention(qda, kda, vda, lams, subln, l, lam_init, nb, T)
        o_na = _na_attention(qna, kna, vna, bias, l, nb, T)
        o_da, o_na = _meta_queries(qda, kda, vda, qna, kna, vna, lams, subln, o_da, o_na, l, lam_init, nb, T)
        h = _merge(o_da, o_na, sga, sgb, h, wa, wb, wo, l)
        h = _peer(h, g_ffn, wqT, keys, u, vT, l)

    gf = norm_final.reshape(1, D_MODEL)
    y1 = _final_norm(h, gf, 0, nb1 * SEQ // TOK).reshape(nb1, SEQ, D_MODEL)
    y2 = _final_norm(h, gf, nb1 * SEQ // TOK, nb2 * SEQ // TOK).reshape(nb2, SEQ, D_MODEL)
    return (y1, y2)
```

```python
import functools
import math

import numpy as np
import jax
import jax.numpy as jnp
from jax import lax
from jax.experimental import pallas as pl
from jax.experimental.pallas import tpu as pltpu

F32 = jnp.float32
BF16 = jnp.bfloat16

D_MODEL = 1024
SEQ = 2048
N_META = 16
GRID_W = 64
ROWS = SEQ // GRID_W
ROPE_THETA = 10000.0
RMS_EPS = 1e-6
LOG2E = math.log2(math.e)
DA_HEADS = 4
DA_HEAD = 64
NA_HEADS = 8
NA_HEAD = 64
NA_KH = 8
NA_KW = 16
NA_COLS = 512
PEER_HEADS = 8
PEER_NKEYS = 128
PEER_TOPK = 16
PEER_EXPERTS = PEER_NKEYS * PEER_NKEYS

LANES = 128
TOK = 512
DA_QB = 256
NA_QR = 4
NA_GROUPS = ROWS // NA_QR
NA_UR = NA_QR + NA_KH - 1
NA_QN = NA_QR * GRID_W
NA_UN = NA_UR * GRID_W
PEER_EC = 1024
PEER_NCH = PEER_EXPERTS // PEER_EC
PEER_NSTEP = PEER_NCH // 2
PEER_IPC = PEER_EC // PEER_NKEYS
GATE_KEYS = 128
GATE_IG = 2
VMEM_LIMIT = 60 * 1024 * 1024

PEER_CAND = tuple((a, b) for a in range(PEER_TOPK) for b in range(PEER_TOPK)
                  if (a + 1) * (b + 1) <= PEER_TOPK)
NCAND = len(PEER_CAND)

NT_DIMS = (((1,), (1,)), ((), ()))


def _round_up(x, m):
    return (x + m - 1) // m * m


def _cparams(sem):
    return pltpu.CompilerParams(dimension_semantics=sem, vmem_limit_bytes=VMEM_LIMIT)


def _rms(x, gain):
    return x * lax.rsqrt(jnp.mean(x * x, axis=-1, keepdims=True) + RMS_EPS) * gain


def _in_proj_kernel(h_ref, g_ref, w_ref, cos_ref, sin_ref,
                    qda_ref, kda_ref, vda_ref, qna_ref, kna_ref, vna_ref, sga_ref, sgb_ref, xn_ref):
    xn_ref[...] = _rms(h_ref[...], g_ref[...]).astype(BF16)

    def mm(c0, n):
        return jnp.dot(xn_ref[...], w_ref[:, c0:c0 + n], preferred_element_type=F32)

    cos = cos_ref[...]
    sin = sin_ref[...]
    qda_ref[...] = ((mm(0, 512) * cos + mm(512, 512) * sin) * (DA_HEAD ** -0.5 * LOG2E)).astype(BF16)
    kda_ref[...] = (mm(1024, 512) * cos + mm(1536, 512) * sin).astype(BF16)
    vda_ref[...] = mm(2048, 512).astype(BF16)
    qna_ref[...] = (mm(2560, 512) * (NA_HEAD ** -0.5 * LOG2E)).astype(BF16)
    kna_ref[...] = mm(3072, 512).astype(BF16)
    vna_ref[...] = mm(3584, 512).astype(BF16)
    sga_ref[...] = jax.nn.sigmoid(mm(4096, 1024)).astype(BF16)
    sgb_ref[...] = jax.nn.sigmoid(mm(5120, 1024)).astype(BF16)


def _in_proj(h, gains, w_ext, cos_t, sin_t, layer, n_real_blocks):
    T = h.shape[0]
    per_seq = SEQ // TOK

    def row(i):
        return (i, 0)

    def tbl(i):
        return (jnp.where(i < n_real_blocks, i % per_seq, per_seq), 0)

    outs = [jax.ShapeDtypeStruct((T, 512), BF16)] * 6 + [jax.ShapeDtypeStruct((T, D_MODEL), BF16)] * 2
    return pl.pallas_call(
        _in_proj_kernel,
        grid=(T // TOK,),
        in_specs=[
            pl.BlockSpec((TOK, D_MODEL), row),
            pl.BlockSpec((None, 1, D_MODEL), lambda i: (layer, 0, 0)),
            pl.BlockSpec((None, D_MODEL, 6144), lambda i: (layer, 0, 0)),
            pl.BlockSpec((TOK, 512), tbl),
            pl.BlockSpec((TOK, 512), tbl),
        ],
        out_specs=[pl.BlockSpec((TOK, 512), row)] * 6 + [pl.BlockSpec((TOK, D_MODEL), row)] * 2,
        out_shape=outs,
        scratch_shapes=[pltpu.VMEM((TOK, D_MODEL), BF16)],
        compiler_params=_cparams(("parallel",)),
        name="in_proj",
    )(h, gains, w_ext, cos_t, sin_t)


def _lambda(lq1_ref, lk1_ref, lq2_ref, lk2_ref, lam_init):
    a = jnp.sum(lq1_ref[...] * lk1_ref[...], axis=-1, keepdims=True)
    b = jnp.sum(lq2_ref[...] * lk2_ref[...], axis=-1, keepdims=True)
    return jnp.exp(a) - jnp.exp(b) + lam_init


def _split_halves(x):
    lane = lax.broadcasted_iota(jnp.int32, x.shape, 1)
    zero = jnp.zeros_like(x)
    return jnp.concatenate([jnp.where(lane < 64, x, zero), jnp.where(lane >= 64, x, zero)], axis=0)


def _diff_attend(q, kr_ref, vr_ref, km_ref, vm_ref, lam, gain, lam_init, o_ref):
    n = q.shape[0]
    for h in range(DA_HEADS):
        sl = slice(h * 128, (h + 1) * 128)
        qq = _split_halves(q[:, sl])
        s = lax.dot_general(qq, kr_ref[:, sl], NT_DIMS, preferred_element_type=F32)
        sm = lax.dot_general(qq, km_ref[:, sl], NT_DIMS, preferred_element_type=F32)
        m = jnp.maximum(jnp.max(s, axis=-1, keepdims=True), jnp.max(sm, axis=-1, keepdims=True))
        p = jnp.exp2(s - m)
        pm = jnp.exp2(sm - m)
        l = jnp.sum(p, axis=-1, keepdims=True) + jnp.sum(pm, axis=-1, keepdims=True)
        r = lam * l[:n] / l[n:]
        a = (p[:n] - p[n:] * r).astype(BF16)
        am = (pm[:n] - pm[n:] * r).astype(BF16)
        o = (jnp.dot(a, vr_ref[:, sl], preferred_element_type=F32)
             + jnp.dot(am, vm_ref[:, sl], preferred_element_type=F32))
        o = _rms(o * (1.0 / l[:n]), gain) * (1.0 - lam_init)
        o_ref[:, sl] = o.astype(BF16)


def _da_kernel(q_ref, kr_ref, vr_ref, km_ref, vm_ref, lq1_ref, lk1_ref, lq2_ref, lk2_ref, sg_ref,
               o_ref, *, lam_init):
    lam = _lambda(lq1_ref, lk1_ref, lq2_ref, lk2_ref, lam_init)
    _diff_attend(q_ref[...], kr_ref, vr_ref, km_ref, vm_ref, lam, sg_ref[...], lam_init, o_ref)


def _lam_specs(layer):
    return [pl.BlockSpec((None, 1, DA_HEAD), lambda *_: (layer, 0, 0))] * 4 + \
           [pl.BlockSpec((None, 1, 2 * DA_HEAD), lambda *_: (layer, 0, 0))]


def _diff_attention(qda, kda, vda, lams, subln, layer, lam_init, nb, T):
    R = nb * SEQ
    qpb = SEQ // DA_QB
    meta_blk0 = R // N_META
    return pl.pallas_call(
        functools.partial(_da_kernel, lam_init=lam_init),
        grid=(nb, qpb),
        in_specs=[
            pl.BlockSpec((DA_QB, 512), lambda b, j: (b * qpb + j, 0)),
            pl.BlockSpec((SEQ, 512), lambda b, j: (b, 0)),
            pl.BlockSpec((SEQ, 512), lambda b, j: (b, 0)),
            pl.BlockSpec((N_META, 512), lambda b, j: (meta_blk0 + b, 0)),
            pl.BlockSpec((N_META, 512), lambda b, j: (meta_blk0 + b, 0)),
        ] + _lam_specs(layer),
        out_specs=pl.BlockSpec((DA_QB, 512), lambda b, j: (b * qpb + j, 0)),
        out_shape=jax.ShapeDtypeStruct((T, 512), BF16),
        compiler_params=_cparams(("parallel", "parallel")),
        name="diff_attn",
    )(qda, kda, vda, kda, vda, *lams, subln)


def _na_union_start(g):
    return min(max(NA_QR * g - NA_KH // 2, 0), ROWS - NA_KH, ROWS - NA_UR)


def _na_table_id(g):
    return 0 if g == 0 else (2 if g == NA_GROUPS - 1 else 1)


def _na_bias_tables(rpb):
    depth = rpb.shape[0]
    qc = np.arange(GRID_W)
    kc = np.arange(GRID_W)
    dc = np.clip(kc[None, :] - qc[:, None], -(NA_KW - 1), NA_KW - 1) + (NA_KW - 1)
    onehot = (dc[None] == np.arange(2 * NA_KW - 1)[:, None, None]).astype(np.float32)
    cs = np.clip(qc - NA_KW // 2, 0, GRID_W - NA_KW)
    col_ok = (kc[None, :] >= cs[:, None]) & (kc[None, :] < cs[:, None] + NA_KW)
    t = jnp.einsum("lhrc,cqk->lhrqk", rpb.astype(F32), jnp.asarray(onehot), precision=lax.Precision.HIGHEST)
    t = jnp.where(jnp.asarray(col_ok), t * LOG2E, -jnp.inf)
    neg = jnp.full((depth, NA_HEADS, GRID_W, GRID_W), -jnp.inf, F32)
    tabs = []
    for g in (0, 1, NA_GROUPS - 1):
        us = _na_union_start(g)
        q_rows = []
        for ri in range(NA_QR):
            r = NA_QR * g + ri
            rs = min(max(r - NA_KH // 2, 0), ROWS - NA_KH)
            blocks = [t[:, :, us + ki - r + NA_KH - 1] if rs <= us + ki < rs + NA_KH else neg
                      for ki in range(NA_UR)]
            q_rows.append(jnp.concatenate(blocks, axis=-1))
        tabs.append(jnp.concatenate(q_rows, axis=-2))
    return jnp.stack(tabs, axis=1)


def _na_kernel(q_ref, k_ref, v_ref, km_ref, vm_ref, bias_ref, o_ref):
    g = pl.program_id(0)
    us = jnp.minimum(jnp.clip(NA_QR * g - NA_KH // 2, 0, ROWS - NA_KH), ROWS - NA_UR)
    start = pl.multiple_of(us * GRID_W, GRID_W)
    for hp in range(NA_HEADS // 2):
        sl = slice(hp * 128, (hp + 1) * 128)
        qq = _split_halves(q_ref[:, sl])
        kw = k_ref[pl.ds(start, NA_UN), sl]
        vw = v_ref[pl.ds(start, NA_UN), sl]
        s = lax.dot_general(qq, kw, NT_DIMS, preferred_element_type=F32)
        s = s + jnp.concatenate([bias_ref[2 * hp], bias_ref[2 * hp + 1]], axis=0)
        sm = lax.dot_general(qq, km_ref[:, sl], NT_DIMS, preferred_element_type=F32)
        m = jnp.maximum(jnp.max(s, axis=-1, keepdims=True), jnp.max(sm, axis=-1, keepdims=True))
        p = jnp.exp2(s - m)
        pm = jnp.exp2(sm - m)
        inv = 1.0 / (jnp.sum(p, axis=-1, keepdims=True) + jnp.sum(pm, axis=-1, keepdims=True))
        o2 = (jnp.dot((p * inv).astype(BF16), vw, preferred_element_type=F32)
              + jnp.dot((pm * inv).astype(BF16), vm_ref[:, sl], preferred_element_type=F32))
        lane = lax.broadcasted_iota(jnp.int32, (NA_QN, 128), 1)
        o_ref[:, sl] = jnp.where(lane < 64, o2[:NA_QN], o2[NA_QN:]).astype(BF16)


def _na_attention(qna, kna, vna, bias, layer, nb, T):
    R = nb * SEQ
    meta_blk0 = R // N_META
    last = NA_GROUPS - 1

    def tid(g, b):
        return (layer, jnp.where(g == 0, 0, jnp.where(g == last, 2, 1)), 0, 0, 0)

    return pl.pallas_call(
        _na_kernel,
        grid=(NA_GROUPS, nb),
        in_specs=[
            pl.BlockSpec((NA_QN, NA_COLS), lambda g, b: (b * NA_GROUPS + g, 0)),
            pl.BlockSpec((SEQ, NA_COLS), lambda g, b: (b, 0)),
            pl.BlockSpec((SEQ, NA_COLS), lambda g, b: (b, 0)),
            pl.BlockSpec((N_META, NA_COLS), lambda g, b: (meta_blk0 + b, 0)),
            pl.BlockSpec((N_META, NA_COLS), lambda g, b: (meta_blk0 + b, 0)),
            pl.BlockSpec((None, None, NA_HEADS, NA_QN, NA_UN), tid),
        ],
        out_specs=pl.BlockSpec((NA_QN, NA_COLS), lambda g, b: (b * NA_GROUPS + g, 0)),
        out_shape=jax.ShapeDtypeStruct((T, NA_COLS), BF16),
        compiler_params=_cparams(("parallel", "parallel")),
        name="nbhd_attn",
    )(qna, kna, vna, kna, vna, bias)


def _meta_kernel(qd_ref, kr_ref, vr_ref, kdm_ref, vdm_ref, qn_ref, knm_ref, vnm_ref,
                 lq1_ref, lk1_ref, lq2_ref, lk2_ref, sg_ref, oda_in, ona_in, oda_ref, ona_ref,
                 *, lam_init, nb):
    del oda_in, ona_in
    i = pl.program_id(0)

    @pl.when(i < nb)
    def _():
        lam = _lambda(lq1_ref, lk1_ref, lq2_ref, lk2_ref, lam_init)
        _diff_attend(qd_ref[...], kr_ref, vr_ref, kdm_ref, vdm_ref, lam, sg_ref[...], lam_init, oda_ref)
        for hp in range(NA_HEADS // 2):
            sl = slice(hp * 128, (hp + 1) * 128)
            qq = _split_halves(qn_ref[:, sl])
            s = lax.dot_general(qq, knm_ref[:, sl], NT_DIMS, preferred_element_type=F32)
            p = jnp.exp2(s - jnp.max(s, axis=-1, keepdims=True))
            p = p * (1.0 / jnp.sum(p, axis=-1, keepdims=True))
            o2 = jnp.dot(p.astype(BF16), vnm_ref[:, sl], preferred_element_type=F32)
            lane = lax.broadcasted_iota(jnp.int32, (N_META, 128), 1)
            ona_ref[:, sl] = jnp.where(lane < 64, o2[:N_META], o2[N_META:]).astype(BF16)

    @pl.when(i >= nb)
    def _():
        oda_ref[...] = jnp.zeros_like(oda_ref)
        ona_ref[...] = jnp.zeros_like(ona_ref)


def _meta_queries(qda, kda, vda, qna, kna, vna, lams, subln, o_da, o_na, layer, lam_init, nb, T):
    R = nb * SEQ
    meta_blk0 = R // N_META
    n_steps = (T - R) // N_META

    def seq(i):
        return (jnp.minimum(i, nb - 1), 0)

    def meta_in(i):
        return (meta_blk0 + jnp.minimum(i, nb - 1), 0)

    def meta_out(i):
        return (meta_blk0 + i, 0)

    mspec = pl.BlockSpec((N_META, 512), meta_in)
    return pl.pallas_call(
        functools.partial(_meta_kernel, lam_init=lam_init, nb=nb),
        grid=(n_steps,),
        in_specs=[mspec, pl.BlockSpec((SEQ, 512), seq), pl.BlockSpec((SEQ, 512), seq), mspec, mspec,
                  mspec, mspec, mspec] + _lam_specs(layer) +
                 [pl.BlockSpec(memory_space=pl.ANY), pl.BlockSpec(memory_space=pl.ANY)],
        out_specs=[pl.BlockSpec((N_META, 512), meta_out), pl.BlockSpec((N_META, 512), meta_out)],
        out_shape=[jax.ShapeDtypeStruct((T, 512), BF16), jax.ShapeDtypeStruct((T, 512), BF16)],
        input_output_aliases={13: 0, 14: 1},
        compiler_params=_cparams(("arbitrary",)),
        name="meta_queries",
    )(qda, kda, vda, kda, vda, qna, kna, vna, *lams, subln, o_da, o_na)


def _merge_kernel(oda_ref, ona_ref, sga_ref, sgb_ref, h_ref, wa_ref, wb_ref, wo_ref, out_ref):
    ya = jnp.dot(oda_ref[...], wa_ref[...], preferred_element_type=F32)
    yb = jnp.dot(ona_ref[...], wb_ref[...], preferred_element_type=F32)
    merged = sga_ref[...].astype(F32) * ya + sgb_ref[...].astype(F32) * yb
    out_ref[...] = h_ref[...] + jnp.dot(merged.astype(BF16), wo_ref[...], preferred_element_type=F32)


def _merge(o_da, o_na, sga, sgb, h, wa, wb, wo, layer):
    T = h.shape[0]

    def row(i):
        return (i, 0)

    def lyr(i):
        return (layer, 0, 0)

    return pl.pallas_call(
        _merge_kernel,
        grid=(T // TOK,),
        in_specs=[
            pl.BlockSpec((TOK, 512), row), pl.BlockSpec((TOK, 512), row),
            pl.BlockSpec((TOK, D_MODEL), row), pl.BlockSpec((TOK, D_MODEL), row),
            pl.BlockSpec((TOK, D_MODEL), row),
            pl.BlockSpec((None, 512, D_MODEL), lyr), pl.BlockSpec((None, 512, D_MODEL), lyr),
            pl.BlockSpec((None, D_MODEL, D_MODEL), lyr),
        ],
        out_specs=pl.BlockSpec((TOK, D_MODEL), row),
        out_shape=jax.ShapeDtypeStruct((T, D_MODEL), F32),
        input_output_aliases={4: 0},
        compiler_params=_cparams(("parallel",)),
        name="merge",
    )(o_da, o_na, sga, sgb, h, wa, wb, wo)


def _gate_chunk(i0, tiles, hid_ref, wT_ref, r2_ref, p2_ref, cnti_ref, w1_ref):
    zero = jnp.zeros((), BF16)
    for t in tiles:
        ls = slice(t * LANES, (t + 1) * LANES)
        for kb in range(PEER_NKEYS // GATE_KEYS):
            ks = slice(kb * GATE_KEYS, (kb + 1) * GATE_KEYS)
            for ig in range(PEER_IPC // GATE_IG):
                acc = [None] * GATE_IG
                for h in range(PEER_HEADS):
                    r2t = r2_ref[h, ks, ls]
                    p2t = p2_ref[h, ks, ls]
                    cnt8 = cnti_ref[h, i0:i0 + PEER_IPC, ls]
                    w8 = w1_ref[h, i0:i0 + PEER_IPC, ls]
                    for q in range(GATE_IG):
                        ii = ig * GATE_IG + q
                        cb = jnp.broadcast_to(cnt8[ii:ii + 1], (16, LANES)).astype(BF16)
                        wb = jnp.broadcast_to(w8[ii:ii + 1], (16, LANES)).astype(BF16)
                        cb = jnp.tile(cb, (GATE_KEYS // 16, 1))
                        wb = jnp.tile(wb, (GATE_KEYS // 16, 1))
                        term = jnp.where(r2t < cb, p2t * wb, zero)
                        acc[q] = term if acc[q] is None else acc[q] + term
                for q in range(GATE_IG):
                    ii = ig * GATE_IG + q
                    rows = slice(ii * PEER_NKEYS + kb * GATE_KEYS, ii * PEER_NKEYS + (kb + 1) * GATE_KEYS)
                    hid = hid_ref[rows, ls]
                    act = 0.5 * hid * (1.0 + lax.erf(hid * math.sqrt(0.5)))
                    wT_ref[rows, ls] = acc[q] * act.astype(BF16)


def _peer_kernel(h_ref, g_ref, wqT_ref, keys_ref, u_ref, vTa_ref, vTb_ref, vTl_ref, out_ref,
                 cT_ref, qT_ref, s_ref, rk_ref, srt_ref, r2_ref, w1_ref, p2_ref, cnti_ref,
                 s1h_ref, s2h_ref, s1s_ref, s2s_ref, cnt_ref, cnth_ref, zinv_ref,
                 hida_ref, hidb_ref, wTa_ref, wTb_ref, acc_ref, cnts_ref, w1s_ref):
    j = pl.program_id(1)
    ntile = TOK // LANES

    def top_ranks(s):
        kiota = lax.broadcasted_iota(jnp.int32, (PEER_NKEYS, TOK), 0).astype(F32)
        riota = lax.broadcasted_iota(jnp.int32, (PEER_TOPK, TOK), 0)
        unranked = jnp.full((PEER_NKEYS, TOK), float(PEER_TOPK), F32)
        srt0 = jnp.zeros((PEER_TOPK, TOK), F32)

        def peel(exact):
            def body(a, srt):
                sv = s_ref[...]
                m = jnp.max(sv, axis=0, keepdims=True)
                hit = sv == m
                if exact:
                    first = jnp.min(jnp.where(hit, kiota, float(PEER_NKEYS)), axis=0, keepdims=True)
                    hit = kiota == first
                s_ref[...] = jnp.where(hit, -jnp.inf, sv)
                rk_ref[...] = jnp.where(hit, lax.convert_element_type(a, F32), rk_ref[...])
                return jnp.where(riota == a, m, srt)
            return body

        s_ref[...] = s
        rk_ref[...] = unranked
        srt_ref[...] = lax.fori_loop(0, PEER_TOPK, peel(False), srt0)
        ranked = jnp.sum(jnp.where(rk_ref[...] < float(PEER_TOPK), 1.0, 0.0), axis=0, keepdims=True)
        tied = jnp.max(ranked) > float(PEER_TOPK)

        @pl.when(tied)
        def _():
            s_ref[...] = s
            rk_ref[...] = unranked
            srt_ref[...] = lax.fori_loop(0, PEER_TOPK, peel(True), srt0)

        return rk_ref[...], srt_ref[...]

    @pl.when(j == 0)
    def _route():
        wTb_ref[...] = jnp.zeros_like(wTb_ref)
        acc_ref[...] = jnp.zeros_like(acc_ref)
        c = _rms(h_ref[...], g_ref[...])
        cT_ref[...] = c.T.astype(BF16)
        qT_ref[...] = jnp.dot(wqT_ref[...], cT_ref[...], preferred_element_type=F32).astype(BF16)

        def per_head(h, carry):
            rows = pl.ds(pl.multiple_of(h * PEER_NKEYS, PEER_NKEYS), PEER_NKEYS)
            q1 = qT_ref[pl.ds(pl.multiple_of(2 * h * PEER_NKEYS, PEER_NKEYS), PEER_NKEYS), :]
            q2 = qT_ref[pl.ds(pl.multiple_of((2 * h + 1) * PEER_NKEYS, PEER_NKEYS), PEER_NKEYS), :]
            s1 = jnp.dot(keys_ref[h, 0], q1, preferred_element_type=F32)
            rk, srt = top_ranks(s1)
            hida_ref[rows, :] = rk
            w1_ref[h] = jnp.exp(s1 - srt[0:1])
            s1h_ref[h] = srt
            s2 = jnp.dot(keys_ref[h, 1], q2, preferred_element_type=F32)
            rk, srt = top_ranks(s2)
            r2_ref[h] = rk.astype(BF16)
            p2_ref[h] = jnp.exp(s2 - srt[0:1]).astype(BF16)
            s2h_ref[h] = srt
            return carry

        lax.fori_loop(0, PEER_HEADS, per_head, 0)

        for h in range(PEER_HEADS):
            for a in range(PEER_TOPK):
                s1s_ref[a, h:h + 1, :] = s1h_ref[h, a:a + 1, :]
                s2s_ref[a, h:h + 1, :] = s2h_ref[h, a:a + 1, :]

        def joint(t, carry):
            ls = pl.ds(pl.multiple_of(t * LANES, LANES), LANES)
            s1 = [s1s_ref[a, :, ls] for a in range(PEER_TOPK)]
            s2 = [s2s_ref[b, :, ls] for b in range(PEER_TOPK)]
            cs = [s1[a] + s2[b] for (a, b) in PEER_CAND]
            ahead = [jnp.zeros((PEER_HEADS, LANES), F32) for _ in range(NCAND)]
            behind = [jnp.zeros((PEER_HEADS, LANES), F32) for _ in range(NCAND)]
            for c in range(NCAND):
                for c2 in range(c):
                    w = jnp.where(cs[c2] >= cs[c], 1.0, 0.0)
                    ahead[c] = ahead[c] + w
                    behind[c2] = behind[c2] + w
            e1 = [jnp.exp(s1[a] - s1[0]) for a in range(PEER_TOPK)]
            e2 = [jnp.exp(s2[b] - s2[0]) for b in range(PEER_TOPK)]
            z = jnp.zeros((PEER_HEADS, LANES), F32)
            cnt = [jnp.zeros((PEER_HEADS, LANES), F32) for _ in range(PEER_TOPK)]
            for c, (a, b) in enumerate(PEER_CAND):
                rank = ahead[c] + (float(NCAND - 1 - c) - behind[c])
                sel = jnp.where(rank < float(PEER_TOPK), 1.0, 0.0)
                cnt[a] = cnt[a] + sel
                z = z + sel * (e1[a] * e2[b])
            for a in range(PEER_TOPK):
                cnt_ref[a, :, ls] = cnt[a]
            zinv_ref[:, ls] = 1.0 / z
            return carry

        lax.fori_loop(0, ntile, joint, 0)

        for h in range(PEER_HEADS):
            for a in range(PEER_TOPK):
                cnth_ref[h, a:a + 1, :] = cnt_ref[a, h:h + 1, :]
            cnth_ref[h, PEER_TOPK:PEER_TOPK + 1, :] = zinv_ref[h:h + 1, :]

        def per_head_counts(h, carry):
            rows = pl.ds(pl.multiple_of(h * PEER_NKEYS, PEER_NKEYS), PEER_NKEYS)
            r1 = hida_ref[rows, :]
            cn = cnth_ref[h, 0:PEER_TOPK, :]
            ci = jnp.zeros((PEER_NKEYS, TOK), F32)
            for a in range(PEER_TOPK):
                ci = jnp.where(r1 == float(a), cn[a:a + 1], ci)
            cnti_ref[h] = ci
            w1_ref[h] = w1_ref[h] * cnth_ref[h, PEER_TOPK:PEER_TOPK + 1, :]
            return carry

        lax.fori_loop(0, PEER_HEADS, per_head_counts, 0)

    half = TOK // 2
    halves = (slice(0, half), slice(half, TOK))
    half_tiles = (tuple(range(0, ntile // 2)), tuple(range(ntile // 2, ntile)))
    step_rows = pl.ds(pl.multiple_of(j * (2 * PEER_IPC), 2 * PEER_IPC), 2 * PEER_IPC)
    for h in range(PEER_HEADS):
        cnts_ref[h] = cnti_ref[h, step_rows, :]
        w1s_ref[h] = w1_ref[h, step_rows, :]
    ia, ib = 0, PEER_IPC

    def mm_hid(hid_ref, urows, hf):
        hid_ref[:, halves[hf]] = jnp.dot(u_ref[urows, :], cT_ref[:, halves[hf]], preferred_element_type=F32)

    def mm_out(vT_ref, wT_ref, hf):
        acc_ref[:, halves[hf]] += jnp.dot(vT_ref[...], wT_ref[:, halves[hf]], preferred_element_type=F32)

    def gate(i0, hid_ref, wT_ref, hf):
        _gate_chunk(i0, half_tiles[hf], hid_ref, wT_ref, r2_ref, p2_ref, cnts_ref, w1s_ref)

    rows_a, rows_b = slice(0, PEER_EC), slice(PEER_EC, 2 * PEER_EC)
    mm_hid(hida_ref, rows_a, 0)
    mm_hid(hida_ref, rows_a, 1)
    mm_out(vTb_ref, wTb_ref, 0)
    gate(ia, hida_ref, wTa_ref, 0)
    mm_out(vTb_ref, wTb_ref, 1)
    mm_hid(hidb_ref, rows_b, 0)
    gate(ia, hida_ref, wTa_ref, 1)
    mm_out(vTa_ref, wTa_ref, 0)
    mm_hid(hidb_ref, rows_b, 1)
    gate(ib, hidb_ref, wTb_ref, 0)
    mm_out(vTa_ref, wTa_ref, 1)
    gate(ib, hidb_ref, wTb_ref, 1)

    @pl.when(j == PEER_NSTEP - 1)
    def _():
        last = jnp.dot(vTl_ref[...], wTb_ref[...], preferred_element_type=F32)
        out_ref[...] = h_ref[...] + (acc_ref[...] + last).T


def _peer(h, gains, wqT, keys, u, vT, layer):
    T = h.shape[0]
    hk = (PEER_HEADS, PEER_NKEYS, TOK)
    once = pl.Buffered(1)
    return pl.pallas_call(
        _peer_kernel,
        grid=(T // TOK, PEER_NSTEP),
        in_specs=[
            pl.BlockSpec((TOK, D_MODEL), lambda t, j: (t, 0)),
            pl.BlockSpec((None, 1, D_MODEL), lambda t, j: (layer, 0, 0)),
            pl.BlockSpec((None, PEER_HEADS * 256, D_MODEL), lambda t, j: (layer, 0, 0), pipeline_mode=once),
            pl.BlockSpec((None, PEER_HEADS, 2, PEER_NKEYS, 128), lambda t, j: (layer, 0, 0, 0, 0),
                         pipeline_mode=once),
            pl.BlockSpec((None, 2 * PEER_EC, D_MODEL), lambda t, j: (layer, j, 0)),
            pl.BlockSpec((None, None, D_MODEL, PEER_EC), lambda t, j: (layer, 2 * j, 0, 0)),
            pl.BlockSpec((None, None, D_MODEL, PEER_EC), lambda t, j: (layer, jnp.maximum(2 * j - 1, 0), 0, 0)),
            pl.BlockSpec((None, None, D_MODEL, PEER_EC), lambda t, j: (layer, PEER_NCH - 1, 0, 0),
                         pipeline_mode=once),
        ],
        out_specs=pl.BlockSpec((TOK, D_MODEL), lambda t, j: (t, 0)),
        out_shape=jax.ShapeDtypeStruct((T, D_MODEL), F32),
        scratch_shapes=[
            pltpu.VMEM((D_MODEL, TOK), BF16),
            pltpu.VMEM((PEER_HEADS * 256, TOK), BF16),
            pltpu.VMEM((PEER_NKEYS, TOK), F32),
            pltpu.VMEM((PEER_NKEYS, TOK), F32),
            pltpu.VMEM((PEER_TOPK, TOK), F32),
            pltpu.VMEM(hk, BF16),
            pltpu.VMEM(hk, F32),
            pltpu.VMEM(hk, BF16),
            pltpu.VMEM(hk, F32),
            pltpu.VMEM((PEER_HEADS, PEER_TOPK, TOK), F32),
            pltpu.VMEM((PEER_HEADS, PEER_TOPK, TOK), F32),
            pltpu.VMEM((PEER_TOPK, PEER_HEADS, TOK), F32),
            pltpu.VMEM((PEER_TOPK, PEER_HEADS, TOK), F32),
            pltpu.VMEM((PEER_TOPK, PEER_HEADS, TOK), F32),
            pltpu.VMEM((PEER_HEADS, PEER_TOPK + 8, TOK), F32),
            pltpu.VMEM((PEER_HEADS, TOK), F32),
            pltpu.VMEM((PEER_EC, TOK), F32),
            pltpu.VMEM((PEER_EC, TOK), F32),
            pltpu.VMEM((PEER_EC, TOK), BF16),
            pltpu.VMEM((PEER_EC, TOK), BF16),
            pltpu.VMEM((D_MODEL, TOK), F32),
            pltpu.VMEM((PEER_HEADS, 2 * PEER_IPC, TOK), F32),
            pltpu.VMEM((PEER_HEADS, 2 * PEER_IPC, TOK), F32),
        ],
        input_output_aliases={0: 0},
        compiler_params=_cparams(("parallel", "arbitrary")),
        name="peer",
    )(h, gains, wqT, keys, u, vT, vT, vT)


def _final_kernel(h_ref, g_ref, o_ref):
    o_ref[...] = _rms(h_ref[...], g_ref[...])


def _final_norm(h, gain, blk0, nblk):
    return pl.pallas_call(
        _final_kernel,
        grid=(nblk,),
        in_specs=[pl.BlockSpec((TOK, D_MODEL), lambda i: (blk0 + i, 0)),
                  pl.BlockSpec((1, D_MODEL), lambda i: (0, 0))],
        out_specs=pl.BlockSpec((TOK, D_MODEL), lambda i: (i, 0)),
        out_shape=jax.ShapeDtypeStruct((nblk * TOK, D_MODEL), F32),
        compiler_params=_cparams(("parallel",)),
        name="final_norm",
    )(h, gain)


def _rope_tables():
    half = DA_HEAD // 2
    inv = 1.0 / (ROPE_THETA ** (jnp.arange(half, dtype=F32) * 2.0 / DA_HEAD))
    pos = jnp.concatenate([jnp.arange(N_META, N_META + SEQ, dtype=F32),
                           jnp.tile(jnp.arange(N_META, dtype=F32), TOK // N_META)])
    ang = pos[:, None] * inv[None, :]
    cos, sin = jnp.cos(ang), jnp.sin(ang)
    cos_t = jnp.tile(cos, (1, 512 // half))
    sin_t = jnp.tile(jnp.concatenate([-sin, sin], axis=1), (1, 512 // DA_HEAD))
    return cos_t, sin_t


def kernel(x_prompt, x_sample, meta_tokens, norm_mix, w_in, lambda_q1, lambda_k1, lambda_q2, lambda_k2,
           subln_gain, na_rpb, w_branch_a, w_branch_b, w_out, norm_ffn, peer_wq, peer_keys, peer_u,
           peer_v, norm_final):
    depth = w_in.shape[0]
    nb1, nb2 = x_prompt.shape[0], x_sample.shape[0]
    nb = nb1 + nb2
    R = nb * SEQ
    T = R + _round_up(nb * N_META, TOK)

    h = jnp.concatenate([
        x_prompt.reshape(nb1 * SEQ, D_MODEL), x_sample.reshape(nb2 * SEQ, D_MODEL),
        jnp.broadcast_to(meta_tokens[None], (nb, N_META, D_MODEL)).reshape(nb * N_META, D_MODEL),
        jnp.zeros((T - R - nb * N_META, D_MODEL), F32)], axis=0)

    swap = np.array([(c // DA_HEAD) * DA_HEAD + (c % DA_HEAD + DA_HEAD // 2) % DA_HEAD for c in range(512)])
    wq, wk = w_in[:, :, :512], w_in[:, :, 512:1024]
    w_ext = jnp.concatenate([wq, wq[:, :, swap], wk, wk[:, :, swap], w_in[:, :, 1024:]], axis=2).astype(BF16)
    cos_t, sin_t = _rope_tables()
    bias = _na_bias_tables(na_rpb)
    wa, wb, wo = w_branch_a.astype(BF16), w_branch_b.astype(BF16), w_out.astype(BF16)
    wqT = jnp.transpose(peer_wq, (0, 2, 1)).astype(BF16)
    keys = peer_keys.astype(BF16)
    u = peer_u.astype(BF16)
    vT = jnp.transpose(peer_v.reshape(depth, PEER_NCH, PEER_EC, D_MODEL), (0, 1, 3, 2)).astype(BF16)
    lams = [x.reshape(depth, 1, DA_HEAD) for x in (lambda_q1, lambda_k1, lambda_q2, lambda_k2)]
    subln = subln_gain.reshape(depth, 1, 2 * DA_HEAD)
    g_mix = norm_mix.reshape(depth, 1, D_MODEL)
    g_ffn = norm_ffn.reshape(depth, 1, D_MODEL)

    for l in range(depth):
        lam_init = 0.8 - 0.6 * math.exp(-0.3 * l)
        qda, kda, vda, qna, kna, vna, sga, sgb = _in_proj(h, g_mix, w_ext, cos_t, sin_t, l, R // TOK)
        o_da = _diff_attention(qda, kda, vda, lams, subln, l, lam_init, nb, T)
        o_na = _na_attention(qna, kna, vna, bias, l, nb, T)
        o_da, o_na = _meta_queries(qda, kda, vda, qna, kna, vna, lams, subln, o_da, o_na, l, lam_init, nb, T)
        h = _merge(o_da, o_na, sga, sgb, h, wa, wb, wo, l)
        h = _peer(h, g_ffn, wqT, keys, u, vT, l)

    gf = norm_final.reshape(1, D_MODEL)
    y1 = _final_norm(h, gf, 0, nb1 * SEQ // TOK).reshape(nb1, SEQ, D_MODEL)
    y2 = _final_norm(h, gf, nb1 * SEQ // TOK, nb2 * SEQ // TOK).reshape(nb2, SEQ, D_MODEL)
    return (y1, y2)
```

```python
import functools
import math

import numpy as np
import jax
import jax.numpy as jnp
from jax import lax
from jax.experimental import pallas as pl
from jax.experimental.pallas import tpu as pltpu

F32 = jnp.float32
BF16 = jnp.bfloat16

D_MODEL = 1024
SEQ = 2048
N_META = 16
GRID_W = 64
ROWS = SEQ // GRID_W
ROPE_THETA = 10000.0
RMS_EPS = 1e-6
LOG2E = math.log2(math.e)
DA_HEADS = 4
DA_HEAD = 64
NA_HEADS = 8
NA_HEAD = 64
NA_KH = 8
NA_KW = 16
NA_COLS = 512
PEER_HEADS = 8
PEER_NKEYS = 128
PEER_TOPK = 16
PEER_EXPERTS = PEER_NKEYS * PEER_NKEYS

LANES = 128
TOK = 512
DA_QB = 512
NA_QR = 4
NA_GROUPS = ROWS // NA_QR
NA_UR = NA_QR + NA_KH - 1
NA_QN = NA_QR * GRID_W
NA_UN = NA_UR * GRID_W
PEER_EC = 1024
PEER_NCH = PEER_EXPERTS // PEER_EC
PEER_NSTEP = PEER_NCH // 2
PEER_IPC = PEER_EC // PEER_NKEYS
GATE_KEYS = 128
GATE_IG = 2
VMEM_LIMIT = 60 * 1024 * 1024

PEER_CAND = tuple((a, b) for a in range(PEER_TOPK) for b in range(PEER_TOPK)
                  if (a + 1) * (b + 1) <= PEER_TOPK)
NCAND = len(PEER_CAND)

NT_DIMS = (((1,), (1,)), ((), ()))


def _round_up(x, m):
    return (x + m - 1) // m * m


def _cparams(sem):
    return pltpu.CompilerParams(dimension_semantics=sem, vmem_limit_bytes=VMEM_LIMIT)


def _rms(x, gain):
    return x * lax.rsqrt(jnp.mean(x * x, axis=-1, keepdims=True) + RMS_EPS) * gain


def _in_proj_kernel(h_ref, g_ref, w_ref, cos_ref, sin_ref,
                    qda_ref, kda_ref, vda_ref, qna_ref, kna_ref, vna_ref, sga_ref, sgb_ref, xn_ref):
    xn_ref[...] = _rms(h_ref[...], g_ref[...]).astype(BF16)

    def mm(c0, n):
        return jnp.dot(xn_ref[...], w_ref[:, c0:c0 + n], preferred_element_type=F32)

    cos = cos_ref[...]
    sin = sin_ref[...]
    qda_ref[...] = ((mm(0, 512) * cos + mm(512, 512) * sin) * (DA_HEAD ** -0.5 * LOG2E)).astype(BF16)
    kda_ref[...] = (mm(1024, 512) * cos + mm(1536, 512) * sin).astype(BF16)
    vda_ref[...] = mm(2048, 512).astype(BF16)
    qna_ref[...] = (mm(2560, 512) * (NA_HEAD ** -0.5 * LOG2E)).astype(BF16)
    kna_ref[...] = mm(3072, 512).astype(BF16)
    vna_ref[...] = mm(3584, 512).astype(BF16)
    sga_ref[...] = jax.nn.sigmoid(mm(4096, 1024)).astype(BF16)
    sgb_ref[...] = jax.nn.sigmoid(mm(5120, 1024)).astype(BF16)


def _in_proj(h, gains, w_ext, cos_t, sin_t, layer, n_real_blocks):
    T = h.shape[0]
    per_seq = SEQ // TOK

    def row(i):
        return (i, 0)

    def tbl(i):
        return (jnp.where(i < n_real_blocks, i % per_seq, per_seq), 0)

    outs = [jax.ShapeDtypeStruct((T, 512), BF16)] * 6 + [jax.ShapeDtypeStruct((T, D_MODEL), BF16)] * 2
    return pl.pallas_call(
        _in_proj_kernel,
        grid=(T // TOK,),
        in_specs=[
            pl.BlockSpec((TOK, D_MODEL), row),
            pl.BlockSpec((None, 1, D_MODEL), lambda i: (layer, 0, 0)),
            pl.BlockSpec((None, D_MODEL, 6144), lambda i: (layer, 0, 0)),
            pl.BlockSpec((TOK, 512), tbl),
            pl.BlockSpec((TOK, 512), tbl),
        ],
        out_specs=[pl.BlockSpec((TOK, 512), row)] * 6 + [pl.BlockSpec((TOK, D_MODEL), row)] * 2,
        out_shape=outs,
        scratch_shapes=[pltpu.VMEM((TOK, D_MODEL), BF16)],
        compiler_params=_cparams(("parallel",)),
        name="in_proj",
    )(h, gains, w_ext, cos_t, sin_t)


def _lambda(lq1_ref, lk1_ref, lq2_ref, lk2_ref, lam_init):
    a = jnp.sum(lq1_ref[...] * lk1_ref[...], axis=-1, keepdims=True)
    b = jnp.sum(lq2_ref[...] * lk2_ref[...], axis=-1, keepdims=True)
    return jnp.exp(a) - jnp.exp(b) + lam_init


def _split_halves(x):
    lane = lax.broadcasted_iota(jnp.int32, x.shape, 1)
    zero = jnp.zeros_like(x)
    return jnp.concatenate([jnp.where(lane < 64, x, zero), jnp.where(lane >= 64, x, zero)], axis=0)


def _diff_attend(q, kr_ref, vr_ref, km_ref, vm_ref, lam, gain, lam_init, o_ref):
    n = q.shape[0]
    for h in range(DA_HEADS):
        sl = slice(h * 128, (h + 1) * 128)
        qq = _split_halves(q[:, sl])
        s = lax.dot_general(qq, kr_ref[:, sl], NT_DIMS, preferred_element_type=F32)
        sm = lax.dot_general(qq, km_ref[:, sl], NT_DIMS, preferred_element_type=F32)
        m = jnp.maximum(jnp.max(s, axis=-1, keepdims=True), jnp.max(sm, axis=-1, keepdims=True))
        p = jnp.exp2(s - m)
        pm = jnp.exp2(sm - m)
        l = jnp.sum(p, axis=-1, keepdims=True) + jnp.sum(pm, axis=-1, keepdims=True)
        r = lam * l[:n] / l[n:]
        a = (p[:n] - p[n:] * r).astype(BF16)
        am = (pm[:n] - pm[n:] * r).astype(BF16)
        o = (jnp.dot(a, vr_ref[:, sl], preferred_element_type=F32)
             + jnp.dot(am, vm_ref[:, sl], preferred_element_type=F32))
        o = _rms(o * (1.0 / l[:n]), gain) * (1.0 - lam_init)
        o_ref[:, sl] = o.astype(BF16)


def _da_kernel(q_ref, kr_ref, vr_ref, km_ref, vm_ref, lq1_ref, lk1_ref, lq2_ref, lk2_ref, sg_ref,
               o_ref, *, lam_init):
    lam = _lambda(lq1_ref, lk1_ref, lq2_ref, lk2_ref, lam_init)
    _diff_attend(q_ref[...], kr_ref, vr_ref, km_ref, vm_ref, lam, sg_ref[...], lam_init, o_ref)


def _lam_specs(layer):
    return [pl.BlockSpec((None, 1, DA_HEAD), lambda *_: (layer, 0, 0))] * 4 + \
           [pl.BlockSpec((None, 1, 2 * DA_HEAD), lambda *_: (layer, 0, 0))]


def _diff_attention(qda, kda, vda, lams, subln, layer, lam_init, nb, T):
    R = nb * SEQ
    qpb = SEQ // DA_QB
    meta_blk0 = R // N_META
    return pl.pallas_call(
        functools.partial(_da_kernel, lam_init=lam_init),
        grid=(nb, qpb),
        in_specs=[
            pl.BlockSpec((DA_QB, 512), lambda b, j: (b * qpb + j, 0)),
            pl.BlockSpec((SEQ, 512), lambda b, j: (b, 0)),
            pl.BlockSpec((SEQ, 512), lambda b, j: (b, 0)),
            pl.BlockSpec((N_META, 512), lambda b, j: (meta_blk0 + b, 0)),
            pl.BlockSpec((N_META, 512), lambda b, j: (meta_blk0 + b, 0)),
        ] + _lam_specs(layer),
        out_specs=pl.BlockSpec((DA_QB, 512), lambda b, j: (b * qpb + j, 0)),
        out_shape=jax.ShapeDtypeStruct((T, 512), BF16),
        compiler_params=_cparams(("parallel", "parallel")),
        name="diff_attn",
    )(qda, kda, vda, kda, vda, *lams, subln)


def _na_union_start(g):
    return min(max(NA_QR * g - NA_KH // 2, 0), ROWS - NA_KH, ROWS - NA_UR)


def _na_table_id(g):
    return 0 if g == 0 else (2 if g == NA_GROUPS - 1 else 1)


def _na_bias_tables(rpb):
    depth = rpb.shape[0]
    qc = np.arange(GRID_W)
    kc = np.arange(GRID_W)
    dc = np.clip(kc[None, :] - qc[:, None], -(NA_KW - 1), NA_KW - 1) + (NA_KW - 1)
    onehot = (dc[None] == np.arange(2 * NA_KW - 1)[:, None, None]).astype(np.float32)
    cs = np.clip(qc - NA_KW // 2, 0, GRID_W - NA_KW)
    col_ok = (kc[None, :] >= cs[:, None]) & (kc[None, :] < cs[:, None] + NA_KW)
    t = jnp.einsum("lhrc,cqk->lhrqk", rpb.astype(F32), jnp.asarray(onehot), precision=lax.Precision.HIGHEST)
    t = jnp.where(jnp.asarray(col_ok), t * LOG2E, -jnp.inf)
    neg = jnp.full((depth, NA_HEADS, GRID_W, GRID_W), -jnp.inf, F32)
    tabs = []
    for g in (0, 1, NA_GROUPS - 1):
        us = _na_union_start(g)
        q_rows = []
        for ri in range(NA_QR):
            r = NA_QR * g + ri
            rs = min(max(r - NA_KH // 2, 0), ROWS - NA_KH)
            blocks = [t[:, :, us + ki - r + NA_KH - 1] if rs <= us + ki < rs + NA_KH else neg
                      for ki in range(NA_UR)]
            q_rows.append(jnp.concatenate(blocks, axis=-1))
        tabs.append(jnp.concatenate(q_rows, axis=-2))
    return jnp.stack(tabs, axis=1)


def _na_kernel(q_ref, k_ref, v_ref, km_ref, vm_ref, bias_ref, o_ref):
    g = pl.program_id(0)
    us = jnp.minimum(jnp.clip(NA_QR * g - NA_KH // 2, 0, ROWS - NA_KH), ROWS - NA_UR)
    start = pl.multiple_of(us * GRID_W, GRID_W)
    for hp in range(NA_HEADS // 2):
        sl = slice(hp * 128, (hp + 1) * 128)
        qq = _split_halves(q_ref[:, sl])
        kw = k_ref[pl.ds(start, NA_UN), sl]
        vw = v_ref[pl.ds(start, NA_UN), sl]
        s = lax.dot_general(qq, kw, NT_DIMS, preferred_element_type=F32)
        s = s + jnp.concatenate([bias_ref[2 * hp], bias_ref[2 * hp + 1]], axis=0)
        sm = lax.dot_general(qq, km_ref[:, sl], NT_DIMS, preferred_element_type=F32)
        m = jnp.maximum(jnp.max(s, axis=-1, keepdims=True), jnp.max(sm, axis=-1, keepdims=True))
        p = jnp.exp2(s - m)
        pm = jnp.exp2(sm - m)
        inv = 1.0 / (jnp.sum(p, axis=-1, keepdims=True) + jnp.sum(pm, axis=-1, keepdims=True))
        o2 = (jnp.dot((p * inv).astype(BF16), vw, preferred_element_type=F32)
              + jnp.dot((pm * inv).astype(BF16), vm_ref[:, sl], preferred_element_type=F32))
        lane = lax.broadcasted_iota(jnp.int32, (NA_QN, 128), 1)
        o_ref[:, sl] = jnp.where(lane < 64, o2[:NA_QN], o2[NA_QN:]).astype(BF16)


def _na_attention(qna, kna, vna, bias, layer, nb, T):
    R = nb * SEQ
    meta_blk0 = R // N_META
    last = NA_GROUPS - 1

    def tid(g, b):
        return (layer, jnp.where(g == 0, 0, jnp.where(g == last, 2, 1)), 0, 0, 0)

    return pl.pallas_call(
        _na_kernel,
        grid=(NA_GROUPS, nb),
        in_specs=[
            pl.BlockSpec((NA_QN, NA_COLS), lambda g, b: (b * NA_GROUPS + g, 0)),
            pl.BlockSpec((SEQ, NA_COLS), lambda g, b: (b, 0)),
            pl.BlockSpec((SEQ, NA_COLS), lambda g, b: (b, 0)),
            pl.BlockSpec((N_META, NA_COLS), lambda g, b: (meta_blk0 + b, 0)),
            pl.BlockSpec((N_META, NA_COLS), lambda g, b: (meta_blk0 + b, 0)),
            pl.BlockSpec((None, None, NA_HEADS, NA_QN, NA_UN), tid),
        ],
        out_specs=pl.BlockSpec((NA_QN, NA_COLS), lambda g, b: (b * NA_GROUPS + g, 0)),
        out_shape=jax.ShapeDtypeStruct((T, NA_COLS), BF16),
        compiler_params=_cparams(("parallel", "parallel")),
        name="nbhd_attn",
    )(qna, kna, vna, kna, vna, bias)


def _meta_kernel(qd_ref, kr_ref, vr_ref, kdm_ref, vdm_ref, qn_ref, knm_ref, vnm_ref,
                 lq1_ref, lk1_ref, lq2_ref, lk2_ref, sg_ref, oda_in, ona_in, oda_ref, ona_ref,
                 *, lam_init, nb):
    del oda_in, ona_in
    i = pl.program_id(0)

    @pl.when(i < nb)
    def _():
        lam = _lambda(lq1_ref, lk1_ref, lq2_ref, lk2_ref, lam_init)
        _diff_attend(qd_ref[...], kr_ref, vr_ref, kdm_ref, vdm_ref, lam, sg_ref[...], lam_init, oda_ref)
        for hp in range(NA_HEADS // 2):
            sl = slice(hp * 128, (hp + 1) * 128)
            qq = _split_halves(qn_ref[:, sl])
            s = lax.dot_general(qq, knm_ref[:, sl], NT_DIMS, preferred_element_type=F32)
            p = jnp.exp2(s - jnp.max(s, axis=-1, keepdims=True))
            p = p * (1.0 / jnp.sum(p, axis=-1, keepdims=True))
            o2 = jnp.dot(p.astype(BF16), vnm_ref[:, sl], preferred_element_type=F32)
            lane = lax.broadcasted_iota(jnp.int32, (N_META, 128), 1)
            ona_ref[:, sl] = jnp.where(lane < 64, o2[:N_META], o2[N_META:]).astype(BF16)

    @pl.when(i >= nb)
    def _():
        oda_ref[...] = jnp.zeros_like(oda_ref)
        ona_ref[...] = jnp.zeros_like(ona_ref)


def _meta_queries(qda, kda, vda, qna, kna, vna, lams, subln, o_da, o_na, layer, lam_init, nb, T):
    R = nb * SEQ
    meta_blk0 = R // N_META
    n_steps = (T - R) // N_META

    def seq(i):
        return (jnp.minimum(i, nb - 1), 0)

    def meta_in(i):
        return (meta_blk0 + jnp.minimum(i, nb - 1), 0)

    def meta_out(i):
        return (meta_blk0 + i, 0)

    mspec = pl.BlockSpec((N_META, 512), meta_in)
    return pl.pallas_call(
        functools.partial(_meta_kernel, lam_init=lam_init, nb=nb),
        grid=(n_steps,),
        in_specs=[mspec, pl.BlockSpec((SEQ, 512), seq), pl.BlockSpec((SEQ, 512), seq), mspec, mspec,
                  mspec, mspec, mspec] + _lam_specs(layer) +
                 [pl.BlockSpec(memory_space=pl.ANY), pl.BlockSpec(memory_space=pl.ANY)],
        out_specs=[pl.BlockSpec((N_META, 512), meta_out), pl.BlockSpec((N_META, 512), meta_out)],
        out_shape=[jax.ShapeDtypeStruct((T, 512), BF16), jax.ShapeDtypeStruct((T, 512), BF16)],
        input_output_aliases={13: 0, 14: 1},
        compiler_params=_cparams(("arbitrary",)),
        name="meta_queries",
    )(qda, kda, vda, kda, vda, qna, kna, vna, *lams, subln, o_da, o_na)


def _merge_kernel(oda_ref, ona_ref, sga_ref, sgb_ref, h_ref, wa_ref, wb_ref, wo_ref, out_ref):
    ya = jnp.dot(oda_ref[...], wa_ref[...], preferred_element_type=F32)
    yb = jnp.dot(ona_ref[...], wb_ref[...], preferred_element_type=F32)
    merged = sga_ref[...].astype(F32) * ya + sgb_ref[...].astype(F32) * yb
    out_ref[...] = h_ref[...] + jnp.dot(merged.astype(BF16), wo_ref[...], preferred_element_type=F32)


def _merge(o_da, o_na, sga, sgb, h, wa, wb, wo, layer):
    T = h.shape[0]

    def row(i):
        return (i, 0)

    def lyr(i):
        return (layer, 0, 0)

    return pl.pallas_call(
        _merge_kernel,
        grid=(T // TOK,),
        in_specs=[
            pl.BlockSpec((TOK, 512), row), pl.BlockSpec((TOK, 512), row),
            pl.BlockSpec((TOK, D_MODEL), row), pl.BlockSpec((TOK, D_MODEL), row),
            pl.BlockSpec((TOK, D_MODEL), row),
            pl.BlockSpec((None, 512, D_MODEL), lyr), pl.BlockSpec((None, 512, D_MODEL), lyr),
            pl.BlockSpec((None, D_MODEL, D_MODEL), lyr),
        ],
        out_specs=pl.BlockSpec((TOK, D_MODEL), row),
        out_shape=jax.ShapeDtypeStruct((T, D_MODEL), F32),
        input_output_aliases={4: 0},
        compiler_params=_cparams(("parallel",)),
        name="merge",
    )(o_da, o_na, sga, sgb, h, wa, wb, wo)


def _gate_chunk(i0, tiles, hid_ref, wT_ref, r2_ref, p2_ref, cnti_ref, w1_ref):
    zero = jnp.zeros((), BF16)
    for t in tiles:
        ls = slice(t * LANES, (t + 1) * LANES)
        for kb in range(PEER_NKEYS // GATE_KEYS):
            ks = slice(kb * GATE_KEYS, (kb + 1) * GATE_KEYS)
            for ig in range(PEER_IPC // GATE_IG):
                acc = [None] * GATE_IG
                for h in range(PEER_HEADS):
                    r2t = r2_ref[h, ks, ls]
                    p2t = p2_ref[h, ks, ls]
                    cnt8 = cnti_ref[h, i0:i0 + PEER_IPC, ls]
                    w8 = w1_ref[h, i0:i0 + PEER_IPC, ls]
                    for q in range(GATE_IG):
                        ii = ig * GATE_IG + q
                        cb = jnp.broadcast_to(cnt8[ii:ii + 1], (16, LANES)).astype(BF16)
                        wb = jnp.broadcast_to(w8[ii:ii + 1], (16, LANES)).astype(BF16)
                        cb = jnp.tile(cb, (GATE_KEYS // 16, 1))
                        wb = jnp.tile(wb, (GATE_KEYS // 16, 1))
                        term = jnp.where(r2t < cb, p2t * wb, zero)
                        acc[q] = term if acc[q] is None else acc[q] + term
                for q in range(GATE_IG):
                    ii = ig * GATE_IG + q
                    rows = slice(ii * PEER_NKEYS + kb * GATE_KEYS, ii * PEER_NKEYS + (kb + 1) * GATE_KEYS)
                    hid = hid_ref[rows, ls]
                    act = 0.5 * hid * (1.0 + lax.erf(hid * math.sqrt(0.5)))
                    wT_ref[rows, ls] = acc[q] * act.astype(BF16)


def _peer_kernel(h_ref, g_ref, wqT_ref, keys_ref, u_ref, vTa_ref, vTb_ref, vTl_ref, out_ref,
                 cT_ref, qT_ref, s_ref, rk_ref, srt_ref, r2_ref, w1_ref, p2_ref, cnti_ref,
                 s1h_ref, s2h_ref, s1s_ref, s2s_ref, cnt_ref, cnth_ref, zinv_ref,
                 hida_ref, hidb_ref, wTa_ref, wTb_ref, acc_ref, cnts_ref, w1s_ref):
    j = pl.program_id(1)
    ntile = TOK // LANES

    def top_ranks(s):
        kiota = lax.broadcasted_iota(jnp.int32, (PEER_NKEYS, TOK), 0).astype(F32)
        riota = lax.broadcasted_iota(jnp.int32, (PEER_TOPK, TOK), 0)
        unranked = jnp.full((PEER_NKEYS, TOK), float(PEER_TOPK), F32)
        srt0 = jnp.zeros((PEER_TOPK, TOK), F32)

        def peel(exact):
            def body(a, srt):
                sv = s_ref[...]
                m = jnp.max(sv, axis=0, keepdims=True)
                hit = sv == m
                if exact:
                    first = jnp.min(jnp.where(hit, kiota, float(PEER_NKEYS)), axis=0, keepdims=True)
                    hit = kiota == first
                s_ref[...] = jnp.where(hit, -jnp.inf, sv)
                rk_ref[...] = jnp.where(hit, lax.convert_element_type(a, F32), rk_ref[...])
                return jnp.where(riota == a, m, srt)
            return body

        s_ref[...] = s
        rk_ref[...] = unranked
        srt_ref[...] = lax.fori_loop(0, PEER_TOPK, peel(False), srt0)
        ranked = jnp.sum(jnp.where(rk_ref[...] < float(PEER_TOPK), 1.0, 0.0), axis=0, keepdims=True)
        tied = jnp.max(ranked) > float(PEER_TOPK)

        @pl.when(tied)
        def _():
            s_ref[...] = s
            rk_ref[...] = unranked
            srt_ref[...] = lax.fori_loop(0, PEER_TOPK, peel(True), srt0)

        return rk_ref[...], srt_ref[...]

    @pl.when(j == 0)
    def _route():
        wTb_ref[...] = jnp.zeros_like(wTb_ref)
        acc_ref[...] = jnp.zeros_like(acc_ref)
        c = _rms(h_ref[...], g_ref[...])
        cT_ref[...] = c.T.astype(BF16)
        qT_ref[...] = jnp.dot(wqT_ref[...], cT_ref[...], preferred_element_type=F32).astype(BF16)

        def per_head(h, carry):
            rows = pl.ds(pl.multiple_of(h * PEER_NKEYS, PEER_NKEYS), PEER_NKEYS)
            q1 = qT_ref[pl.ds(pl.multiple_of(2 * h * PEER_NKEYS, PEER_NKEYS), PEER_NKEYS), :]
            q2 = qT_ref[pl.ds(pl.multiple_of((2 * h + 1) * PEER_NKEYS, PEER_NKEYS), PEER_NKEYS), :]
            s1 = jnp.dot(keys_ref[h, 0], q1, preferred_element_type=F32)
            rk, srt = top_ranks(s1)
            hida_ref[rows, :] = rk
            w1_ref[h] = jnp.exp(s1 - srt[0:1])
            s1h_ref[h] = srt
            s2 = jnp.dot(keys_ref[h, 1], q2, preferred_element_type=F32)
            rk, srt = top_ranks(s2)
            r2_ref[h] = rk.astype(BF16)
            p2_ref[h] = jnp.exp(s2 - srt[0:1]).astype(BF16)
            s2h_ref[h] = srt
            return carry

        lax.fori_loop(0, PEER_HEADS, per_head, 0)

        for h in range(PEER_HEADS):
            for a in range(PEER_TOPK):
                s1s_ref[a, h:h + 1, :] = s1h_ref[h, a:a + 1, :]
                s2s_ref[a, h:h + 1, :] = s2h_ref[h, a:a + 1, :]

        def joint(t, carry):
            ls = pl.ds(pl.multiple_of(t * LANES, LANES), LANES)
            s1 = [s1s_ref[a, :, ls] for a in range(PEER_TOPK)]
            s2 = [s2s_ref[b, :, ls] for b in range(PEER_TOPK)]
            cs = [s1[a] + s2[b] for (a, b) in PEER_CAND]
            ahead = [jnp.zeros((PEER_HEADS, LANES), F32) for _ in range(NCAND)]
            behind = [jnp.zeros((PEER_HEADS, LANES), F32) for _ in range(NCAND)]
            for c in range(NCAND):
                for c2 in range(c):
                    w = jnp.where(cs[c2] >= cs[c], 1.0, 0.0)
                    ahead[c] = ahead[c] + w
                    behind[c2] = behind[c2] + w
            e1 = [jnp.exp(s1[a] - s1[0]) for a in range(PEER_TOPK)]
            e2 = [jnp.exp(s2[b] - s2[0]) for b in range(PEER_TOPK)]
            z = jnp.zeros((PEER_HEADS, LANES), F32)
            cnt = [jnp.zeros((PEER_HEADS, LANES), F32) for _ in range(PEER_TOPK)]
            for c, (a, b) in enumerate(PEER_CAND):
                rank = ahead[c] + (float(NCAND - 1 - c) - behind[c])
                sel = jnp.where(rank < float(PEER_TOPK), 1.0, 0.0)
                cnt[a] = cnt[a] + sel
                z = z + sel * (e1[a] * e2[b])
            for a in range(PEER_TOPK):
                cnt_ref[a, :, ls] = cnt[a]
            zinv_ref[:, ls] = 1.0 / z
            return carry

        lax.fori_loop(0, ntile, joint, 0)

        for h in range(PEER_HEADS):
            for a in range(PEER_TOPK):
                cnth_ref[h, a:a + 1, :] = cnt_ref[a, h:h + 1, :]
            cnth_ref[h, PEER_TOPK:PEER_TOPK + 1, :] = zinv_ref[h:h + 1, :]

        def per_head_counts(h, carry):
            rows = pl.ds(pl.multiple_of(h * PEER_NKEYS, PEER_NKEYS), PEER_NKEYS)
            r1 = hida_ref[rows, :]
            cn = cnth_ref[h, 0:PEER_TOPK, :]
            ci = jnp.zeros((PEER_NKEYS, TOK), F32)
            for a in range(PEER_TOPK):
                ci = jnp.where(r1 == float(a), cn[a:a + 1], ci)
            cnti_ref[h] = ci
            w1_ref[h] = w1_ref[h] * cnth_ref[h, PEER_TOPK:PEER_TOPK + 1, :]
            return carry

        lax.fori_loop(0, PEER_HEADS, per_head_counts, 0)

    half = TOK // 2
    halves = (slice(0, half), slice(half, TOK))
    half_tiles = (tuple(range(0, ntile // 2)), tuple(range(ntile // 2, ntile)))
    step_rows = pl.ds(pl.multiple_of(j * (2 * PEER_IPC), 2 * PEER_IPC), 2 * PEER_IPC)
    for h in range(PEER_HEADS):
        cnts_ref[h] = cnti_ref[h, step_rows, :]
        w1s_ref[h] = w1_ref[h, step_rows, :]
    ia, ib = 0, PEER_IPC

    def mm_hid(hid_ref, urows, hf):
        hid_ref[:, halves[hf]] = jnp.dot(u_ref[urows, :], cT_ref[:, halves[hf]], preferred_element_type=F32)

    def mm_out(vT_ref, wT_ref, hf):
        acc_ref[:, halves[hf]] += jnp.dot(vT_ref[...], wT_ref[:, halves[hf]], preferred_element_type=F32)

    def gate(i0, hid_ref, wT_ref, hf):
        _gate_chunk(i0, half_tiles[hf], hid_ref, wT_ref, r2_ref, p2_ref, cnts_ref, w1s_ref)

    rows_a, rows_b = slice(0, PEER_EC), slice(PEER_EC, 2 * PEER_EC)
    mm_hid(hida_ref, rows_a, 0)
    mm_hid(hida_ref, rows_a, 1)
    mm_out(vTb_ref, wTb_ref, 0)
    gate(ia, hida_ref, wTa_ref, 0)
    mm_out(vTb_ref, wTb_ref, 1)
    mm_hid(hidb_ref, rows_b, 0)
    gate(ia, hida_ref, wTa_ref, 1)
    mm_out(vTa_ref, wTa_ref, 0)
    mm_hid(hidb_ref, rows_b, 1)
    gate(ib, hidb_ref, wTb_ref, 0)
    mm_out(vTa_ref, wTa_ref, 1)
    gate(ib, hidb_ref, wTb_ref, 1)

    @pl.when(j == PEER_NSTEP - 1)
    def _():
        last = jnp.dot(vTl_ref[...], wTb_ref[...], preferred_element_type=F32)
        out_ref[...] = h_ref[...] + (acc_ref[...] + last).T


def _peer(h, gains, wqT, keys, u, vT, layer):
    T = h.shape[0]
    hk = (PEER_HEADS, PEER_NKEYS, TOK)
    once = pl.Buffered(1)
    return pl.pallas_call(
        _peer_kernel,
        grid=(T // TOK, PEER_NSTEP),
        in_specs=[
            pl.BlockSpec((TOK, D_MODEL), lambda t, j: (t, 0)),
            pl.BlockSpec((None, 1, D_MODEL), lambda t, j: (layer, 0, 0)),
            pl.BlockSpec((None, PEER_HEADS * 256, D_MODEL), lambda t, j: (layer, 0, 0), pipeline_mode=once),
            pl.BlockSpec((None, PEER_HEADS, 2, PEER_NKEYS, 128), lambda t, j: (layer, 0, 0, 0, 0),
                         pipeline_mode=once),
            pl.BlockSpec((None, 2 * PEER_EC, D_MODEL), lambda t, j: (layer, j, 0)),
            pl.BlockSpec((None, None, D_MODEL, PEER_EC), lambda t, j: (layer, 2 * j, 0, 0)),
            pl.BlockSpec((None, None, D_MODEL, PEER_EC), lambda t, j: (layer, jnp.maximum(2 * j - 1, 0), 0, 0)),
            pl.BlockSpec((None, None, D_MODEL, PEER_EC), lambda t, j: (layer, PEER_NCH - 1, 0, 0),
                         pipeline_mode=once),
        ],
        out_specs=pl.BlockSpec((TOK, D_MODEL), lambda t, j: (t, 0)),
        out_shape=jax.ShapeDtypeStruct((T, D_MODEL), F32),
        scratch_shapes=[
            pltpu.VMEM((D_MODEL, TOK), BF16),
            pltpu.VMEM((PEER_HEADS * 256, TOK), BF16),
            pltpu.VMEM((PEER_NKEYS, TOK), F32),
            pltpu.VMEM((PEER_NKEYS, TOK), F32),
            pltpu.VMEM((PEER_TOPK, TOK), F32),
            pltpu.VMEM(hk, BF16),
            pltpu.VMEM(hk, F32),
            pltpu.VMEM(hk, BF16),
            pltpu.VMEM(hk, F32),
            pltpu.VMEM((PEER_HEADS, PEER_TOPK, TOK), F32),
            pltpu.VMEM((PEER_HEADS, PEER_TOPK, TOK), F32),
            pltpu.VMEM((PEER_TOPK, PEER_HEADS, TOK), F32),
            pltpu.VMEM((PEER_TOPK, PEER_HEADS, TOK), F32),
            pltpu.VMEM((PEER_TOPK, PEER_HEADS, TOK), F32),
            pltpu.VMEM((PEER_HEADS, PEER_TOPK + 8, TOK), F32),
            pltpu.VMEM((PEER_HEADS, TOK), F32),
            pltpu.VMEM((PEER_EC, TOK), F32),
            pltpu.VMEM((PEER_EC, TOK), F32),
            pltpu.VMEM((PEER_EC, TOK), BF16),
            pltpu.VMEM((PEER_EC, TOK), BF16),
            pltpu.VMEM((D_MODEL, TOK), F32),
            pltpu.VMEM((PEER_HEADS, 2 * PEER_IPC, TOK), F32),
            pltpu.VMEM((PEER_HEADS, 2 * PEER_IPC, TOK), F32),
        ],
        input_output_aliases={0: 0},
        compiler_params=_cparams(("parallel", "arbitrary")),
        name="peer",
    )(h, gains, wqT, keys, u, vT, vT, vT)


def _final_kernel(h_ref, g_ref, o_ref):
    o_ref[...] = _rms(h_ref[...], g_ref[...])


def _final_norm(h, gain, blk0, nblk):
    return pl.pallas_call(
        _final_kernel,
        grid=(nblk,),
        in_specs=[pl.BlockSpec((TOK, D_MODEL), lambda i: (blk0 + i, 0)),
                  pl.BlockSpec((1, D_MODEL), lambda i: (0, 0))],
        out_specs=pl.BlockSpec((TOK, D_MODEL), lambda i: (i, 0)),
        out_shape=jax.ShapeDtypeStruct((nblk * TOK, D_MODEL), F32),
        compiler_params=_cparams(("parallel",)),
        name="final_norm",
    )(h, gain)


def _rope_tables():
    half = DA_HEAD // 2
    inv = 1.0 / (ROPE_THETA ** (jnp.arange(half, dtype=F32) * 2.0 / DA_HEAD))
    pos = jnp.concatenate([jnp.arange(N_META, N_META + SEQ, dtype=F32),
                           jnp.tile(jnp.arange(N_META, dtype=F32), TOK // N_META)])
    ang = pos[:, None] * inv[None, :]
    cos, sin = jnp.cos(ang), jnp.sin(ang)
    cos_t = jnp.tile(cos, (1, 512 // half))
    sin_t = jnp.tile(jnp.concatenate([-sin, sin], axis=1), (1, 512 // DA_HEAD))
    return cos_t, sin_t


def kernel(x_prompt, x_sample, meta_tokens, norm_mix, w_in, lambda_q1, lambda_k1, lambda_q2, lambda_k2,
           subln_gain, na_rpb, w_branch_a, w_branch_b, w_out, norm_ffn, peer_wq, peer_keys, peer_u,
           peer_v, norm_final):
    depth = w_in.shape[0]
    nb1, nb2 = x_prompt.shape[0], x_sample.shape[0]
    nb = nb1 + nb2
    R = nb * SEQ
    T = R + _round_up(nb * N_META, TOK)

    h = jnp.concatenate([
        x_prompt.reshape(nb1 * SEQ, D_MODEL), x_sample.reshape(nb2 * SEQ, D_MODEL),
        jnp.broadcast_to(meta_tokens[None], (nb, N_META, D_MODEL)).reshape(nb * N_META, D_MODEL),
        jnp.zeros((T - R - nb * N_META, D_MODEL), F32)], axis=0)

    swap = np.array([(c // DA_HEAD) * DA_HEAD + (c % DA_HEAD + DA_HEAD // 2) % DA_HEAD for c in range(512)])
    wq, wk = w_in[:, :, :512], w_in[:, :, 512:1024]
    w_ext = jnp.concatenate([wq, wq[:, :, swap], wk, wk[:, :, swap], w_in[:, :, 1024:]], axis=2).astype(BF16)
    cos_t, sin_t = _rope_tables()
    bias = _na_bias_tables(na_rpb)
    wa, wb, wo = w_branch_a.astype(BF16), w_branch_b.astype(BF16), w_out.astype(BF16)
    wqT = jnp.transpose(peer_wq, (0, 2, 1)).astype(BF16)
    keys = peer_keys.astype(BF16)
    u = peer_u.astype(BF16)
    vT = jnp.transpose(peer_v.reshape(depth, PEER_NCH, PEER_EC, D_MODEL), (0, 1, 3, 2)).astype(BF16)
    lams = [x.reshape(depth, 1, DA_HEAD) for x in (lambda_q1, lambda_k1, lambda_q2, lambda_k2)]
    subln = subln_gain.reshape(depth, 1, 2 * DA_HEAD)
    g_mix = norm_mix.reshape(depth, 1, D_MODEL)
    g_ffn = norm_ffn.reshape(depth, 1, D_MODEL)

    for l in range(depth):
        lam_init = 0.8 - 0.6 * math.exp(-0.3 * l)
        qda, kda, vda, qna, kna, vna, sga, sgb = _in_proj(h, g_mix, w_ext, cos_t, sin_t, l, R // TOK)
        o_da = _diff_attention(qda, kda, vda, lams, subln, l, lam_init, nb, T)
        o_na = _na_attention(qna, kna, vna, bias, l, nb, T)
        o_da, o_na = _meta_queries(qda, kda, vda, qna, kna, vna, lams, subln, o_da, o_na, l, lam_init, nb, T)
        h = _merge(o_da, o_na, sga, sgb, h, wa, wb, wo, l)
        h = _peer(h, g_ffn, wqT, keys, u, vT, l)

    gf = norm_final.reshape(1, D_MODEL)
    y1 = _final_norm(h, gf, 0, nb1 * SEQ // TOK).reshape(nb1, SEQ, D_MODEL)
    y2 = _final_norm(h, gf, nb1 * SEQ // TOK, nb2 * SEQ // TOK).reshape(nb2, SEQ, D_MODEL)
    return (y1, y2)
```

```python
import functools
import math

import numpy as np
import jax
import jax.numpy as jnp
from jax import lax
from jax.experimental import pallas as pl
from jax.experimental.pallas import tpu as pltpu

F32 = jnp.float32
BF16 = jnp.bfloat16

D_MODEL = 1024
SEQ = 2048
N_META = 16
GRID_W = 64
ROWS = SEQ // GRID_W
ROPE_THETA = 10000.0
RMS_EPS = 1e-6
LOG2E = math.log2(math.e)
DA_HEADS = 4
DA_HEAD = 64
NA_HEADS = 8
NA_HEAD = 64
NA_KH = 8
NA_KW = 16
NA_COLS = 512
PEER_HEADS = 8
PEER_NKEYS = 128
PEER_TOPK = 16
PEER_EXPERTS = PEER_NKEYS * PEER_NKEYS

LANES = 128
TOK = 512
DA_QB = 1024
NA_QR = 4
NA_GROUPS = ROWS // NA_QR
NA_UR = NA_QR + NA_KH - 1
NA_QN = NA_QR * GRID_W
NA_UN = NA_UR * GRID_W
PEER_EC = 1024
PEER_NCH = PEER_EXPERTS // PEER_EC
PEER_NSTEP = PEER_NCH // 2
PEER_IPC = PEER_EC // PEER_NKEYS
GATE_KEYS = 128
GATE_IG = 2
VMEM_LIMIT = 60 * 1024 * 1024

PEER_CAND = tuple((a, b) for a in range(PEER_TOPK) for b in range(PEER_TOPK)
                  if (a + 1) * (b + 1) <= PEER_TOPK)
NCAND = len(PEER_CAND)

NT_DIMS = (((1,), (1,)), ((), ()))


def _round_up(x, m):
    return (x + m - 1) // m * m


def _cparams(sem):
    return pltpu.CompilerParams(dimension_semantics=sem, vmem_limit_bytes=VMEM_LIMIT)


def _rms(x, gain):
    return x * lax.rsqrt(jnp.mean(x * x, axis=-1, keepdims=True) + RMS_EPS) * gain


def _in_proj_kernel(h_ref, g_ref, w_ref, cos_ref, sin_ref,
                    qda_ref, kda_ref, vda_ref, qna_ref, kna_ref, vna_ref, sga_ref, sgb_ref, xn_ref):
    xn_ref[...] = _rms(h_ref[...], g_ref[...]).astype(BF16)

    def mm(c0, n):
        return jnp.dot(xn_ref[...], w_ref[:, c0:c0 + n], preferred_element_type=F32)

    cos = cos_ref[...]
    sin = sin_ref[...]
    qda_ref[...] = ((mm(0, 512) * cos + mm(512, 512) * sin) * (DA_HEAD ** -0.5 * LOG2E)).astype(BF16)
    kda_ref[...] = (mm(1024, 512) * cos + mm(1536, 512) * sin).astype(BF16)
    vda_ref[...] = mm(2048, 512).astype(BF16)
    qna_ref[...] = (mm(2560, 512) * (NA_HEAD ** -0.5 * LOG2E)).astype(BF16)
    kna_ref[...] = mm(3072, 512).astype(BF16)
    vna_ref[...] = mm(3584, 512).astype(BF16)
    sga_ref[...] = jax.nn.sigmoid(mm(4096, 1024)).astype(BF16)
    sgb_ref[...] = jax.nn.sigmoid(mm(5120, 1024)).astype(BF16)


def _in_proj(h, gains, w_ext, cos_t, sin_t, layer, n_real_blocks):
    T = h.shape[0]
    per_seq = SEQ // TOK

    def row(i):
        return (i, 0)

    def tbl(i):
        return (jnp.where(i < n_real_blocks, i % per_seq, per_seq), 0)

    outs = [jax.ShapeDtypeStruct((T, 512), BF16)] * 6 + [jax.ShapeDtypeStruct((T, D_MODEL), BF16)] * 2
    return pl.pallas_call(
        _in_proj_kernel,
        grid=(T // TOK,),
        in_specs=[
            pl.BlockSpec((TOK, D_MODEL), row),
            pl.BlockSpec((None, 1, D_MODEL), lambda i: (layer, 0, 0)),
            pl.BlockSpec((None, D_MODEL, 6144), lambda i: (layer, 0, 0)),
            pl.BlockSpec((TOK, 512), tbl),
            pl.BlockSpec((TOK, 512), tbl),
        ],
        out_specs=[pl.BlockSpec((TOK, 512), row)] * 6 + [pl.BlockSpec((TOK, D_MODEL), row)] * 2,
        out_shape=outs,
        scratch_shapes=[pltpu.VMEM((TOK, D_MODEL), BF16)],
        compiler_params=_cparams(("parallel",)),
        name="in_proj",
    )(h, gains, w_ext, cos_t, sin_t)


def _lambda(lq1_ref, lk1_ref, lq2_ref, lk2_ref, lam_init):
    a = jnp.sum(lq1_ref[...] * lk1_ref[...], axis=-1, keepdims=True)
    b = jnp.sum(lq2_ref[...] * lk2_ref[...], axis=-1, keepdims=True)
    return jnp.exp(a) - jnp.exp(b) + lam_init


def _split_halves(x):
    lane = lax.broadcasted_iota(jnp.int32, x.shape, 1)
    zero = jnp.zeros_like(x)
    return jnp.concatenate([jnp.where(lane < 64, x, zero), jnp.where(lane >= 64, x, zero)], axis=0)


def _diff_attend(q, kr_ref, vr_ref, km_ref, vm_ref, lam, gain, lam_init, o_ref):
    n = q.shape[0]
    for h in range(DA_HEADS):
        sl = slice(h * 128, (h + 1) * 128)
        qq = _split_halves(q[:, sl])
        s = lax.dot_general(qq, kr_ref[:, sl], NT_DIMS, preferred_element_type=F32)
        sm = lax.dot_general(qq, km_ref[:, sl], NT_DIMS, preferred_element_type=F32)
        m = jnp.maximum(jnp.max(s, axis=-1, keepdims=True), jnp.max(sm, axis=-1, keepdims=True))
        p = jnp.exp2(s - m)
        pm = jnp.exp2(sm - m)
        l = jnp.sum(p, axis=-1, keepdims=True) + jnp.sum(pm, axis=-1, keepdims=True)
        r = lam * l[:n] / l[n:]
        a = (p[:n] - p[n:] * r).astype(BF16)
        am = (pm[:n] - pm[n:] * r).astype(BF16)
        o = (jnp.dot(a, vr_ref[:, sl], preferred_element_type=F32)
             + jnp.dot(am, vm_ref[:, sl], preferred_element_type=F32))
        o = _rms(o * (1.0 / l[:n]), gain) * (1.0 - lam_init)
        o_ref[:, sl] = o.astype(BF16)


def _da_kernel(q_ref, kr_ref, vr_ref, km_ref, vm_ref, lq1_ref, lk1_ref, lq2_ref, lk2_ref, sg_ref,
               o_ref, *, lam_init):
    lam = _lambda(lq1_ref, lk1_ref, lq2_ref, lk2_ref, lam_init)
    _diff_attend(q_ref[...], kr_ref, vr_ref, km_ref, vm_ref, lam, sg_ref[...], lam_init, o_ref)


def _lam_specs(layer):
    return [pl.BlockSpec((None, 1, DA_HEAD), lambda *_: (layer, 0, 0))] * 4 + \
           [pl.BlockSpec((None, 1, 2 * DA_HEAD), lambda *_: (layer, 0, 0))]


def _diff_attention(qda, kda, vda, lams, subln, layer, lam_init, nb, T):
    R = nb * SEQ
    qpb = SEQ // DA_QB
    meta_blk0 = R // N_META
    return pl.pallas_call(
        functools.partial(_da_kernel, lam_init=lam_init),
        grid=(nb, qpb),
        in_specs=[
            pl.BlockSpec((DA_QB, 512), lambda b, j: (b * qpb + j, 0)),
            pl.BlockSpec((SEQ, 512), lambda b, j: (b, 0)),
            pl.BlockSpec((SEQ, 512), lambda b, j: (b, 0)),
            pl.BlockSpec((N_META, 512), lambda b, j: (meta_blk0 + b, 0)),
            pl.BlockSpec((N_META, 512), lambda b, j: (meta_blk0 + b, 0)),
        ] + _lam_specs(layer),
        out_specs=pl.BlockSpec((DA_QB, 512), lambda b, j: (b * qpb + j, 0)),
        out_shape=jax.ShapeDtypeStruct((T, 512), BF16),
        compiler_params=_cparams(("parallel", "parallel")),
        name="diff_attn",
    )(qda, kda, vda, kda, vda, *lams, subln)


def _na_union_start(g):
    return min(max(NA_QR * g - NA_KH // 2, 0), ROWS - NA_KH, ROWS - NA_UR)


def _na_table_id(g):
    return 0 if g == 0 else (2 if g == NA_GROUPS - 1 else 1)


def _na_bias_tables(rpb):
    depth = rpb.shape[0]
    qc = np.arange(GRID_W)
    kc = np.arange(GRID_W)
    dc = np.clip(kc[None, :] - qc[:, None], -(NA_KW - 1), NA_KW - 1) + (NA_KW - 1)
    onehot = (dc[None] == np.arange(2 * NA_KW - 1)[:, None, None]).astype(np.float32)
    cs = np.clip(qc - NA_KW // 2, 0, GRID_W - NA_KW)
    col_ok = (kc[None, :] >= cs[:, None]) & (kc[None, :] < cs[:, None] + NA_KW)
    t = jnp.einsum("lhrc,cqk->lhrqk", rpb.astype(F32), jnp.asarray(onehot), precision=lax.Precision.HIGHEST)
    t = jnp.where(jnp.asarray(col_ok), t * LOG2E, -jnp.inf)
    neg = jnp.full((depth, NA_HEADS, GRID_W, GRID_W), -jnp.inf, F32)
    tabs = []
    for g in (0, 1, NA_GROUPS - 1):
        us = _na_union_start(g)
        q_rows = []
        for ri in range(NA_QR):
            r = NA_QR * g + ri
            rs = min(max(r - NA_KH // 2, 0), ROWS - NA_KH)
            blocks = [t[:, :, us + ki - r + NA_KH - 1] if rs <= us + ki < rs + NA_KH else neg
                      for ki in range(NA_UR)]
            q_rows.append(jnp.concatenate(blocks, axis=-1))
        tabs.append(jnp.concatenate(q_rows, axis=-2))
    return jnp.stack(tabs, axis=1)


def _na_kernel(q_ref, k_ref, v_ref, km_ref, vm_ref, bias_ref, o_ref):
    g = pl.program_id(0)
    us = jnp.minimum(jnp.clip(NA_QR * g - NA_KH // 2, 0, ROWS - NA_KH), ROWS - NA_UR)
    start = pl.multiple_of(us * GRID_W, GRID_W)
    for hp in range(NA_HEADS // 2):
        sl = slice(hp * 128, (hp + 1) * 128)
        qq = _split_halves(q_ref[:, sl])
        kw = k_ref[pl.ds(start, NA_UN), sl]
        vw = v_ref[pl.ds(start, NA_UN), sl]
        s = lax.dot_general(qq, kw, NT_DIMS, preferred_element_type=F32)
        s = s + jnp.concatenate([bias_ref[2 * hp], bias_ref[2 * hp + 1]], axis=0)
        sm = lax.dot_general(qq, km_ref[:, sl], NT_DIMS, preferred_element_type=F32)
        m = jnp.maximum(jnp.max(s, axis=-1, keepdims=True), jnp.max(sm, axis=-1, keepdims=True))
        p = jnp.exp2(s - m)
        pm = jnp.exp2(sm - m)
        inv = 1.0 / (jnp.sum(p, axis=-1, keepdims=True) + jnp.sum(pm, axis=-1, keepdims=True))
        o2 = (jnp.dot((p * inv).astype(BF16), vw, preferred_element_type=F32)
              + jnp.dot((pm * inv).astype(BF16), vm_ref[:, sl], preferred_element_type=F32))
        lane = lax.broadcasted_iota(jnp.int32, (NA_QN, 128), 1)
        o_ref[:, sl] = jnp.where(lane < 64, o2[:NA_QN], o2[NA_QN:]).astype(BF16)


def _na_attention(qna, kna, vna, bias, layer, nb, T):
    R = nb * SEQ
    meta_blk0 = R // N_META
    last = NA_GROUPS - 1

    def tid(g, b):
        return (layer, jnp.where(g == 0, 0, jnp.where(g == last, 2, 1)), 0, 0, 0)

    return pl.pallas_call(
        _na_kernel,
        grid=(NA_GROUPS, nb),
        in_specs=[
            pl.BlockSpec((NA_QN, NA_COLS), lambda g, b: (b * NA_GROUPS + g, 0)),
            pl.BlockSpec((SEQ, NA_COLS), lambda g, b: (b, 0)),
            pl.BlockSpec((SEQ, NA_COLS), lambda g, b: (b, 0)),
            pl.BlockSpec((N_META, NA_COLS), lambda g, b: (meta_blk0 + b, 0)),
            pl.BlockSpec((N_META, NA_COLS), lambda g, b: (meta_blk0 + b, 0)),
            pl.BlockSpec((None, None, NA_HEADS, NA_QN, NA_UN), tid),
        ],
        out_specs=pl.BlockSpec((NA_QN, NA_COLS), lambda g, b: (b * NA_GROUPS + g, 0)),
        out_shape=jax.ShapeDtypeStruct((T, NA_COLS), BF16),
        compiler_params=_cparams(("parallel", "parallel")),
        name="nbhd_attn",
    )(qna, kna, vna, kna, vna, bias)


def _meta_kernel(qd_ref, kr_ref, vr_ref, kdm_ref, vdm_ref, qn_ref, knm_ref, vnm_ref,
                 lq1_ref, lk1_ref, lq2_ref, lk2_ref, sg_ref, oda_in, ona_in, oda_ref, ona_ref,
                 *, lam_init, nb):
    del oda_in, ona_in
    i = pl.program_id(0)

    @pl.when(i < nb)
    def _():
        lam = _lambda(lq1_ref, lk1_ref, lq2_ref, lk2_ref, lam_init)
        _diff_attend(qd_ref[...], kr_ref, vr_ref, kdm_ref, vdm_ref, lam, sg_ref[...], lam_init, oda_ref)
        for hp in range(NA_HEADS // 2):
            sl = slice(hp * 128, (hp + 1) * 128)
            qq = _split_halves(qn_ref[:, sl])
            s = lax.dot_general(qq, knm_ref[:, sl], NT_DIMS, preferred_element_type=F32)
            p = jnp.exp2(s - jnp.max(s, axis=-1, keepdims=True))
            p = p * (1.0 / jnp.sum(p, axis=-1, keepdims=True))
            o2 = jnp.dot(p.astype(BF16), vnm_ref[:, sl], preferred_element_type=F32)
            lane = lax.broadcasted_iota(jnp.int32, (N_META, 128), 1)
            ona_ref[:, sl] = jnp.where(lane < 64, o2[:N_META], o2[N_META:]).astype(BF16)

    @pl.when(i >= nb)
    def _():
        oda_ref[...] = jnp.zeros_like(oda_ref)
        ona_ref[...] = jnp.zeros_like(ona_ref)


def _meta_queries(qda, kda, vda, qna, kna, vna, lams, subln, o_da, o_na, layer, lam_init, nb, T):
    R = nb * SEQ
    meta_blk0 = R // N_META
    n_steps = (T - R) // N_META

    def seq(i):
        return (jnp.minimum(i, nb - 1), 0)

    def meta_in(i):
        return (meta_blk0 + jnp.minimum(i, nb - 1), 0)

    def meta_out(i):
        return (meta_blk0 + i, 0)

    mspec = pl.BlockSpec((N_META, 512), meta_in)
    return pl.pallas_call(
        functools.partial(_meta_kernel, lam_init=lam_init, nb=nb),
        grid=(n_steps,),
        in_specs=[mspec, pl.BlockSpec((SEQ, 512), seq), pl.BlockSpec((SEQ, 512), seq), mspec, mspec,
                  mspec, mspec, mspec] + _lam_specs(layer) +
                 [pl.BlockSpec(memory_space=pl.ANY), pl.BlockSpec(memory_space=pl.ANY)],
        out_specs=[pl.BlockSpec((N_META, 512), meta_out), pl.BlockSpec((N_META, 512), meta_out)],
        out_shape=[jax.ShapeDtypeStruct((T, 512), BF16), jax.ShapeDtypeStruct((T, 512), BF16)],
        input_output_aliases={13: 0, 14: 1},
        compiler_params=_cparams(("arbitrary",)),
        name="meta_queries",
    )(qda, kda, vda, kda, vda, qna, kna, vna, *lams, subln, o_da, o_na)


def _merge_kernel(oda_ref, ona_ref, sga_ref, sgb_ref, h_ref, wa_ref, wb_ref, wo_ref, out_ref):
    ya = jnp.dot(oda_ref[...], wa_ref[...], preferred_element_type=F32)
    yb = jnp.dot(ona_ref[...], wb_ref[...], preferred_element_type=F32)
    merged = sga_ref[...].astype(F32) * ya + sgb_ref[...].astype(F32) * yb
    out_ref[...] = h_ref[...] + jnp.dot(merged.astype(BF16), wo_ref[...], preferred_element_type=F32)


def _merge(o_da, o_na, sga, sgb, h, wa, wb, wo, layer):
    T = h.shape[0]

    def row(i):
        return (i, 0)

    def lyr(i):
        return (layer, 0, 0)

    return pl.pallas_call(
        _merge_kernel,
        grid=(T // TOK,),
        in_specs=[
            pl.BlockSpec((TOK, 512), row), pl.BlockSpec((TOK, 512), row),
            pl.BlockSpec((TOK, D_MODEL), row), pl.BlockSpec((TOK, D_MODEL), row),
            pl.BlockSpec((TOK, D_MODEL), row),
            pl.BlockSpec((None, 512, D_MODEL), lyr), pl.BlockSpec((None, 512, D_MODEL), lyr),
            pl.BlockSpec((None, D_MODEL, D_MODEL), lyr),
        ],
        out_specs=pl.BlockSpec((TOK, D_MODEL), row),
        out_shape=jax.ShapeDtypeStruct((T, D_MODEL), F32),
        input_output_aliases={4: 0},
        compiler_params=_cparams(("parallel",)),
        name="merge",
    )(o_da, o_na, sga, sgb, h, wa, wb, wo)


def _gate_chunk(i0, tiles, hid_ref, wT_ref, r2_ref, p2_ref, cnti_ref, w1_ref):
    zero = jnp.zeros((), BF16)
    for t in tiles:
        ls = slice(t * LANES, (t + 1) * LANES)
        for kb in range(PEER_NKEYS // GATE_KEYS):
            ks = slice(kb * GATE_KEYS, (kb + 1) * GATE_KEYS)
            for ig in range(PEER_IPC // GATE_IG):
                acc = [None] * GATE_IG
                for h in range(PEER_HEADS):
                    r2t = r2_ref[h, ks, ls]
                    p2t = p2_ref[h, ks, ls]
                    cnt8 = cnti_ref[h, i0:i0 + PEER_IPC, ls]
                    w8 = w1_ref[h, i0:i0 + PEER_IPC, ls]
                    for q in range(GATE_IG):
                        ii = ig * GATE_IG + q
                        cb = jnp.broadcast_to(cnt8[ii:ii + 1], (16, LANES)).astype(BF16)
                        wb = jnp.broadcast_to(w8[ii:ii + 1], (16, LANES)).astype(BF16)
                        cb = jnp.tile(cb, (GATE_KEYS // 16, 1))
                        wb = jnp.tile(wb, (GATE_KEYS // 16, 1))
                        term = jnp.where(r2t < cb, p2t * wb, zero)
                        acc[q] = term if acc[q] is None else acc[q] + term
                for q in range(GATE_IG):
                    ii = ig * GATE_IG + q
                    rows = slice(ii * PEER_NKEYS + kb * GATE_KEYS, ii * PEER_NKEYS + (kb + 1) * GATE_KEYS)
                    hid = hid_ref[rows, ls]
                    act = 0.5 * hid * (1.0 + lax.erf(hid * math.sqrt(0.5)))
                    wT_ref[rows, ls] = acc[q] * act.astype(BF16)


def _peer_kernel(h_ref, g_ref, wqT_ref, keys_ref, u_ref, vTa_ref, vTb_ref, vTl_ref, out_ref,
                 cT_ref, qT_ref, s_ref, rk_ref, srt_ref, r2_ref, w1_ref, p2_ref, cnti_ref,
                 s1h_ref, s2h_ref, s1s_ref, s2s_ref, cnt_ref, cnth_ref, zinv_ref,
                 hida_ref, hidb_ref, wTa_ref, wTb_ref, acc_ref, cnts_ref, w1s_ref):
    j = pl.program_id(1)
    ntile = TOK // LANES

    def top_ranks(s):
        kiota = lax.broadcasted_iota(jnp.int32, (PEER_NKEYS, TOK), 0).astype(F32)
        riota = lax.broadcasted_iota(jnp.int32, (PEER_TOPK, TOK), 0)
        unranked = jnp.full((PEER_NKEYS, TOK), float(PEER_TOPK), F32)
        srt0 = jnp.zeros((PEER_TOPK, TOK), F32)

        def peel(exact):
            def body(a, srt):
                sv = s_ref[...]
                m = jnp.max(sv, axis=0, keepdims=True)
                hit = sv == m
                if exact:
                    first = jnp.min(jnp.where(hit, kiota, float(PEER_NKEYS)), axis=0, keepdims=True)
                    hit = kiota == first
                s_ref[...] = jnp.where(hit, -jnp.inf, sv)
                rk_ref[...] = jnp.where(hit, lax.convert_element_type(a, F32), rk_ref[...])
                return jnp.where(riota == a, m, srt)
            return body

        s_ref[...] = s
        rk_ref[...] = unranked
        srt_ref[...] = lax.fori_loop(0, PEER_TOPK, peel(False), srt0)
        ranked = jnp.sum(jnp.where(rk_ref[...] < float(PEER_TOPK), 1.0, 0.0), axis=0, keepdims=True)
        tied = jnp.max(ranked) > float(PEER_TOPK)

        @pl.when(tied)
        def _():
            s_ref[...] = s
            rk_ref[...] = unranked
            srt_ref[...] = lax.fori_loop(0, PEER_TOPK, peel(True), srt0)

        return rk_ref[...], srt_ref[...]

    @pl.when(j == 0)
    def _route():
        wTb_ref[...] = jnp.zeros_like(wTb_ref)
        acc_ref[...] = jnp.zeros_like(acc_ref)
        c = _rms(h_ref[...], g_ref[...])
        cT_ref[...] = c.T.astype(BF16)
        qT_ref[...] = jnp.dot(wqT_ref[...], cT_ref[...], preferred_element_type=F32).astype(BF16)

        def per_head(h, carry):
            rows = pl.ds(pl.multiple_of(h * PEER_NKEYS, PEER_NKEYS), PEER_NKEYS)
            q1 = qT_ref[pl.ds(pl.multiple_of(2 * h * PEER_NKEYS, PEER_NKEYS), PEER_NKEYS), :]
            q2 = qT_ref[pl.ds(pl.multiple_of((2 * h + 1) * PEER_NKEYS, PEER_NKEYS), PEER_NKEYS), :]
            s1 = jnp.dot(keys_ref[h, 0], q1, preferred_element_type=F32)
            rk, srt = top_ranks(s1)
            hida_ref[rows, :] = rk
            w1_ref[h] = jnp.exp(s1 - srt[0:1])
            s1h_ref[h] = srt
            s2 = jnp.dot(keys_ref[h, 1], q2, preferred_element_type=F32)
            rk, srt = top_ranks(s2)
            r2_ref[h] = rk.astype(BF16)
            p2_ref[h] = jnp.exp(s2 - srt[0:1]).astype(BF16)
            s2h_ref[h] = srt
            return carry

        lax.fori_loop(0, PEER_HEADS, per_head, 0)

        for h in range(PEER_HEADS):
            for a in range(PEER_TOPK):
                s1s_ref[a, h:h + 1, :] = s1h_ref[h, a:a + 1, :]
                s2s_ref[a, h:h + 1, :] = s2h_ref[h, a:a + 1, :]

        def joint(t, carry):
            ls = pl.ds(pl.multiple_of(t * LANES, LANES), LANES)
            s1 = [s1s_ref[a, :, ls] for a in range(PEER_TOPK)]
            s2 = [s2s_ref[b, :, ls] for b in range(PEER_TOPK)]
            cs = [s1[a] + s2[b] for (a, b) in PEER_CAND]
            ahead = [jnp.zeros((PEER_HEADS, LANES), F32) for _ in range(NCAND)]
            behind = [jnp.zeros((PEER_HEADS, LANES), F32) for _ in range(NCAND)]
            for c in range(NCAND):
                for c2 in range(c):
                    w = jnp.where(cs[c2] >= cs[c], 1.0, 0.0)
                    ahead[c] = ahead[c] + w
                    behind[c2] = behind[c2] + w
            e1 = [jnp.exp(s1[a] - s1[0]) for a in range(PEER_TOPK)]
            e2 = [jnp.exp(s2[b] - s2[0]) for b in range(PEER_TOPK)]
            z = jnp.zeros((PEER_HEADS, LANES), F32)
            cnt = [jnp.zeros((PEER_HEADS, LANES), F32) for _ in range(PEER_TOPK)]
            for c, (a, b) in enumerate(PEER_CAND):
                rank = ahead[c] + (float(NCAND - 1 - c) - behind[c])
                sel = jnp.where(rank < float(PEER_TOPK), 1.0, 0.0)
                cnt[a] = cnt[a] + sel
                z = z + sel * (e1[a] * e2[b])
            for a in range(PEER_TOPK):
                cnt_ref[a, :, ls] = cnt[a]
            zinv_ref[:, ls] = 1.0 / z
            return carry

        lax.fori_loop(0, ntile, joint, 0)

        for h in range(PEER_HEADS):
            for a in range(PEER_TOPK):
                cnth_ref[h, a:a + 1, :] = cnt_ref[a, h:h + 1, :]
            cnth_ref[h, PEER_TOPK:PEER_TOPK + 1, :] = zinv_ref[h:h + 1, :]

        def per_head_counts(h, carry):
            rows = pl.ds(pl.multiple_of(h * PEER_NKEYS, PEER_NKEYS), PEER_NKEYS)
            r1 = hida_ref[rows, :]
            cn = cnth_ref[h, 0:PEER_TOPK, :]
            ci = jnp.zeros((PEER_NKEYS, TOK), F32)
            for a in range(PEER_TOPK):
                ci = jnp.where(r1 == float(a), cn[a:a + 1], ci)
            cnti_ref[h] = ci
            w1_ref[h] = w1_ref[h] * cnth_ref[h, PEER_TOPK:PEER_TOPK + 1, :]
            return carry

        lax.fori_loop(0, PEER_HEADS, per_head_counts, 0)

    half = TOK // 2
    halves = (slice(0, half), slice(half, TOK))
    half_tiles = (tuple(range(0, ntile // 2)), tuple(range(ntile // 2, ntile)))
    step_rows = pl.ds(pl.multiple_of(j * (2 * PEER_IPC), 2 * PEER_IPC), 2 * PEER_IPC)
    for h in range(PEER_HEADS):
        cnts_ref[h] = cnti_ref[h, step_rows, :]
        w1s_ref[h] = w1_ref[h, step_rows, :]
    ia, ib = 0, PEER_IPC

    def mm_hid(hid_ref, urows, hf):
        hid_ref[:, halves[hf]] = jnp.dot(u_ref[urows, :], cT_ref[:, halves[hf]], preferred_element_type=F32)

    def mm_out(vT_ref, wT_ref, hf):
        acc_ref[:, halves[hf]] += jnp.dot(vT_ref[...], wT_ref[:, halves[hf]], preferred_element_type=F32)

    def gate(i0, hid_ref, wT_ref, hf):
        _gate_chunk(i0, half_tiles[hf], hid_ref, wT_ref, r2_ref, p2_ref, cnts_ref, w1s_ref)

    rows_a, rows_b = slice(0, PEER_EC), slice(PEER_EC, 2 * PEER_EC)
    mm_hid(hida_ref, rows_a, 0)
    mm_hid(hida_ref, rows_a, 1)
    mm_out(vTb_ref, wTb_ref, 0)
    gate(ia, hida_ref, wTa_ref, 0)
    mm_out(vTb_ref, wTb_ref, 1)
    mm_hid(hidb_ref, rows_b, 0)
    gate(ia, hida_ref, wTa_ref, 1)
    mm_out(vTa_ref, wTa_ref, 0)
    mm_hid(hidb_ref, rows_b, 1)
    gate(ib, hidb_ref, wTb_ref, 0)
    mm_out(vTa_ref, wTa_ref, 1)
    gate(ib, hidb_ref, wTb_ref, 1)

    @pl.when(j == PEER_NSTEP - 1)
    def _():
        last = jnp.dot(vTl_ref[...], wTb_ref[...], preferred_element_type=F32)
        out_ref[...] = h_ref[...] + (acc_ref[...] + last).T


def _peer(h, gains, wqT, keys, u, vT, layer):
    T = h.shape[0]
    hk = (PEER_HEADS, PEER_NKEYS, TOK)
    once = pl.Buffered(1)
    return pl.pallas_call(
        _peer_kernel,
        grid=(T // TOK, PEER_NSTEP),
        in_specs=[
            pl.BlockSpec((TOK, D_MODEL), lambda t, j: (t, 0)),
            pl.BlockSpec((None, 1, D_MODEL), lambda t, j: (layer, 0, 0)),
            pl.BlockSpec((None, PEER_HEADS * 256, D_MODEL), lambda t, j: (layer, 0, 0), pipeline_mode=once),
            pl.BlockSpec((None, PEER_HEADS, 2, PEER_NKEYS, 128), lambda t, j: (layer, 0, 0, 0, 0),
                         pipeline_mode=once),
            pl.BlockSpec((None, 2 * PEER_EC, D_MODEL), lambda t, j: (layer, j, 0)),
            pl.BlockSpec((None, None, D_MODEL, PEER_EC), lambda t, j: (layer, 2 * j, 0, 0)),
            pl.BlockSpec((None, None, D_MODEL, PEER_EC), lambda t, j: (layer, jnp.maximum(2 * j - 1, 0), 0, 0)),
            pl.BlockSpec((None, None, D_MODEL, PEER_EC), lambda t, j: (layer, PEER_NCH - 1, 0, 0),
                         pipeline_mode=once),
        ],
        out_specs=pl.BlockSpec((TOK, D_MODEL), lambda t, j: (t, 0)),
        out_shape=jax.ShapeDtypeStruct((T, D_MODEL), F32),
        scratch_shapes=[
            pltpu.VMEM((D_MODEL, TOK), BF16),
            pltpu.VMEM((PEER_HEADS * 256, TOK), BF16),
            pltpu.VMEM((PEER_NKEYS, TOK), F32),
            pltpu.VMEM((PEER_NKEYS, TOK), F32),
            pltpu.VMEM((PEER_TOPK, TOK), F32),
            pltpu.VMEM(hk, BF16),
            pltpu.VMEM(hk, F32),
            pltpu.VMEM(hk, BF16),
            pltpu.VMEM(hk, F32),
            pltpu.VMEM((PEER_HEADS, PEER_TOPK, TOK), F32),
            pltpu.VMEM((PEER_HEADS, PEER_TOPK, TOK), F32),
            pltpu.VMEM((PEER_TOPK, PEER_HEADS, TOK), F32),
            pltpu.VMEM((PEER_TOPK, PEER_HEADS, TOK), F32),
            pltpu.VMEM((PEER_TOPK, PEER_HEADS, TOK), F32),
            pltpu.VMEM((PEER_HEADS, PEER_TOPK + 8, TOK), F32),
            pltpu.VMEM((PEER_HEADS, TOK), F32),
            pltpu.VMEM((PEER_EC, TOK), F32),
            pltpu.VMEM((PEER_EC, TOK), F32),
            pltpu.VMEM((PEER_EC, TOK), BF16),
            pltpu.VMEM((PEER_EC, TOK), BF16),
            pltpu.VMEM((D_MODEL, TOK), F32),
            pltpu.VMEM((PEER_HEADS, 2 * PEER_IPC, TOK), F32),
            pltpu.VMEM((PEER_HEADS, 2 * PEER_IPC, TOK), F32),
        ],
        input_output_aliases={0: 0},
        compiler_params=_cparams(("parallel", "arbitrary")),
        name="peer",
    )(h, gains, wqT, keys, u, vT, vT, vT)


def _final_kernel(h_ref, g_ref, o_ref):
    o_ref[...] = _rms(h_ref[...], g_ref[...])


def _final_norm(h, gain, blk0, nblk):
    return pl.pallas_call(
        _final_kernel,
        grid=(nblk,),
        in_specs=[pl.BlockSpec((TOK, D_MODEL), lambda i: (blk0 + i, 0)),
                  pl.BlockSpec((1, D_MODEL), lambda i: (0, 0))],
        out_specs=pl.BlockSpec((TOK, D_MODEL), lambda i: (i, 0)),
        out_shape=jax.ShapeDtypeStruct((nblk * TOK, D_MODEL), F32),
        compiler_params=_cparams(("parallel",)),
        name="final_norm",
    )(h, gain)


def _rope_tables():
    half = DA_HEAD // 2
    inv = 1.0 / (ROPE_THETA ** (jnp.arange(half, dtype=F32) * 2.0 / DA_HEAD))
    pos = jnp.concatenate([jnp.arange(N_META, N_META + SEQ, dtype=F32),
                           jnp.tile(jnp.arange(N_META, dtype=F32), TOK // N_META)])
    ang = pos[:, None] * inv[None, :]
    cos, sin = jnp.cos(ang), jnp.sin(ang)
    cos_t = jnp.tile(cos, (1, 512 // half))
    sin_t = jnp.tile(jnp.concatenate([-sin, sin], axis=1), (1, 512 // DA_HEAD))
    return cos_t, sin_t


def kernel(x_prompt, x_sample, meta_tokens, norm_mix, w_in, lambda_q1, lambda_k1, lambda_q2, lambda_k2,
           subln_gain, na_rpb, w_branch_a, w_branch_b, w_out, norm_ffn, peer_wq, peer_keys, peer_u,
           peer_v, norm_final):
    depth = w_in.shape[0]
    nb1, nb2 = x_prompt.shape[0], x_sample.shape[0]
    nb = nb1 + nb2
    R = nb * SEQ
    T = R + _round_up(nb * N_META, TOK)

    h = jnp.concatenate([
        x_prompt.reshape(nb1 * SEQ, D_MODEL), x_sample.reshape(nb2 * SEQ, D_MODEL),
        jnp.broadcast_to(meta_tokens[None], (nb, N_META, D_MODEL)).reshape(nb * N_META, D_MODEL),
        jnp.zeros((T - R - nb * N_META, D_MODEL), F32)], axis=0)

    swap = np.array([(c // DA_HEAD) * DA_HEAD + (c % DA_HEAD + DA_HEAD // 2) % DA_HEAD for c in range(512)])
    wq, wk = w_in[:, :, :512], w_in[:, :, 512:1024]
    w_ext = jnp.concatenate([wq, wq[:, :, swap], wk, wk[:, :, swap], w_in[:, :, 1024:]], axis=2).astype(BF16)
    cos_t, sin_t = _rope_tables()
    bias = _na_bias_tables(na_rpb)
    wa, wb, wo = w_branch_a.astype(BF16), w_branch_b.astype(BF16), w_out.astype(BF16)
    wqT = jnp.transpose(peer_wq, (0, 2, 1)).astype(BF16)
    keys = peer_keys.astype(BF16)
    u = peer_u.astype(BF16)
    vT = jnp.transpose(peer_v.reshape(depth, PEER_NCH, PEER_EC, D_MODEL), (0, 1, 3, 2)).astype(BF16)
    lams = [x.reshape(depth, 1, DA_HEAD) for x in (lambda_q1, lambda_k1, lambda_q2, lambda_k2)]
    subln = subln_gain.reshape(depth, 1, 2 * DA_HEAD)
    g_mix = norm_mix.reshape(depth, 1, D_MODEL)
    g_ffn = norm_ffn.reshape(depth, 1, D_MODEL)

    for l in range(depth):
        lam_init = 0.8 - 0.6 * math.exp(-0.3 * l)
        qda, kda, vda, qna, kna, vna, sga, sgb = _in_proj(h, g_mix, w_ext, cos_t, sin_t, l, R // TOK)
        o_da = _diff_attention(qda, kda, vda, lams, subln, l, lam_init, nb, T)
        o_na = _na_attention(qna, kna, vna, bias, l, nb, T)
        o_da, o_na = _meta_queries(qda, kda, vda, qna, kna, vna, lams, subln, o_da, o_na, l, lam_init, nb, T)
        h = _merge(o_da, o_na, sga, sgb, h, wa, wb, wo, l)
        h = _peer(h, g_ffn, wqT, keys, u, vT, l)

    gf = norm_final.reshape(1, D_MODEL)
    y1 = _final_norm(h, gf, 0, nb1 * SEQ // TOK).reshape(nb1, SEQ, D_MODEL)
    y2 = _final_norm(h, gf, nb1 * SEQ // TOK, nb2 * SEQ // TOK).reshape(nb2, SEQ, D_MODEL)
    return (y1, y2)
```

```python
import functools
import math

import numpy as np
import jax
import jax.numpy as jnp
from jax import lax
from jax.experimental import pallas as pl
from jax.experimental.pallas import tpu as pltpu

F32 = jnp.float32
BF16 = jnp.bfloat16

D_MODEL = 1024
SEQ = 2048
N_META = 16
GRID_W = 64
ROWS = SEQ // GRID_W
ROPE_THETA = 10000.0
RMS_EPS = 1e-6
LOG2E = math.log2(math.e)
DA_HEADS = 4
DA_HEAD = 64
NA_HEADS = 8
NA_HEAD = 64
NA_KH = 8
NA_KW = 16
NA_COLS = 512
PEER_HEADS = 8
PEER_NKEYS = 128
PEER_TOPK = 16
PEER_EXPERTS = PEER_NKEYS * PEER_NKEYS

LANES = 128
TOK = 512
DA_QB = 1024
NA_QR = 4
NA_GROUPS = ROWS // NA_QR
NA_UR = NA_QR + NA_KH - 1
NA_QN = NA_QR * GRID_W
NA_UN = NA_UR * GRID_W
PEER_EC = 1024
PEER_NCH = PEER_EXPERTS // PEER_EC
PEER_NSTEP = PEER_NCH // 2
PEER_IPC = PEER_EC // PEER_NKEYS
GATE_KEYS = 128
GATE_IG = 2
VMEM_LIMIT = 60 * 1024 * 1024

PEER_CAND = tuple((a, b) for a in range(PEER_TOPK) for b in range(PEER_TOPK)
                  if (a + 1) * (b + 1) <= PEER_TOPK)
NCAND = len(PEER_CAND)

NT_DIMS = (((1,), (1,)), ((), ()))


def _round_up(x, m):
    return (x + m - 1) // m * m


def _cparams(sem):
    return pltpu.CompilerParams(dimension_semantics=sem, vmem_limit_bytes=VMEM_LIMIT)


def _rms(x, gain):
    return x * lax.rsqrt(jnp.mean(x * x, axis=-1, keepdims=True) + RMS_EPS) * gain


def _in_proj_kernel(h_ref, g_ref, w_ref, cos_ref, sin_ref,
                    qda_ref, kda_ref, vda_ref, qna_ref, kna_ref, vna_ref, sga_ref, sgb_ref, xn_ref):
    xn_ref[...] = _rms(h_ref[...], g_ref[...]).astype(BF16)

    def mm(c0, n):
        return jnp.dot(xn_ref[...], w_ref[:, c0:c0 + n], preferred_element_type=F32)

    cos = cos_ref[...]
    sin = sin_ref[...]
    qda_ref[...] = ((mm(0, 512) * cos + mm(512, 512) * sin) * (DA_HEAD ** -0.5 * LOG2E)).astype(BF16)
    kda_ref[...] = (mm(1024, 512) * cos + mm(1536, 512) * sin).astype(BF16)
    vda_ref[...] = mm(2048, 512).astype(BF16)
    qna_ref[...] = (mm(2560, 512) * (NA_HEAD ** -0.5 * LOG2E)).astype(BF16)
    kna_ref[...] = mm(3072, 512).astype(BF16)
    vna_ref[...] = mm(3584, 512).astype(BF16)
    sga_ref[...] = jax.nn.sigmoid(mm(4096, 1024)).astype(BF16)
    sgb_ref[...] = jax.nn.sigmoid(mm(5120, 1024)).astype(BF16)


def _in_proj(h, gains, w_ext, cos_t, sin_t, layer, n_real_blocks):
    T = h.shape[0]
    per_seq = SEQ // TOK

    def row(i):
        return (i, 0)

    def tbl(i):
        return (jnp.where(i < n_real_blocks, i % per_seq, per_seq), 0)

    outs = [jax.ShapeDtypeStruct((T, 512), BF16)] * 6 + [jax.ShapeDtypeStruct((T, D_MODEL), BF16)] * 2
    return pl.pallas_call(
        _in_proj_kernel,
        grid=(T // TOK,),
        in_specs=[
            pl.BlockSpec((TOK, D_MODEL), row),
            pl.BlockSpec((None, 1, D_MODEL), lambda i: (layer, 0, 0)),
            pl.BlockSpec((None, D_MODEL, 6144), lambda i: (layer, 0, 0)),
            pl.BlockSpec((TOK, 512), tbl),
            pl.BlockSpec((TOK, 512), tbl),
        ],
        out_specs=[pl.BlockSpec((TOK, 512), row)] * 6 + [pl.BlockSpec((TOK, D_MODEL), row)] * 2,
        out_shape=outs,
        scratch_shapes=[pltpu.VMEM((TOK, D_MODEL), BF16)],
        compiler_params=_cparams(("parallel",)),
        name="in_proj",
    )(h, gains, w_ext, cos_t, sin_t)


def _lambda(lq1_ref, lk1_ref, lq2_ref, lk2_ref, lam_init):
    a = jnp.sum(lq1_ref[...] * lk1_ref[...], axis=-1, keepdims=True)
    b = jnp.sum(lq2_ref[...] * lk2_ref[...], axis=-1, keepdims=True)
    return jnp.exp(a) - jnp.exp(b) + lam_init


def _split_halves(x):
    lane = lax.broadcasted_iota(jnp.int32, x.shape, 1)
    zero = jnp.zeros_like(x)
    return jnp.concatenate([jnp.where(lane < 64, x, zero), jnp.where(lane >= 64, x, zero)], axis=0)


def _diff_attend(q, kr_ref, vr_ref, km_ref, vm_ref, lam, gain, lam_init, o_ref):
    n = q.shape[0]
    for h in range(DA_HEADS):
        sl = slice(h * 128, (h + 1) * 128)
        qq = _split_halves(q[:, sl])
        s = lax.dot_general(qq, kr_ref[:, sl], NT_DIMS, preferred_element_type=F32)
        sm = lax.dot_general(qq, km_ref[:, sl], NT_DIMS, preferred_element_type=F32)
        m = jnp.maximum(jnp.max(s, axis=-1, keepdims=True), jnp.max(sm, axis=-1, keepdims=True))
        p = jnp.exp2(s - m)
        pm = jnp.exp2(sm - m)
        l = jnp.sum(p, axis=-1, keepdims=True) + jnp.sum(pm, axis=-1, keepdims=True)
        r = lam * l[:n] / l[n:]
        a = (p[:n] - p[n:] * r).astype(BF16)
        am = (pm[:n] - pm[n:] * r).astype(BF16)
        o = (jnp.dot(a, vr_ref[:, sl], preferred_element_type=F32)
             + jnp.dot(am, vm_ref[:, sl], preferred_element_type=F32))
        o = _rms(o * (1.0 / l[:n]), gain) * (1.0 - lam_init)
        o_ref[:, sl] = o.astype(BF16)


def _da_kernel(q_ref, kr_ref, vr_ref, km_ref, vm_ref, lq1_ref, lk1_ref, lq2_ref, lk2_ref, sg_ref,
               o_ref, *, lam_init):
    lam = _lambda(lq1_ref, lk1_ref, lq2_ref, lk2_ref, lam_init)
    _diff_attend(q_ref[...], kr_ref, vr_ref, km_ref, vm_ref, lam, sg_ref[...], lam_init, o_ref)


def _lam_specs(layer):
    return [pl.BlockSpec((None, 1, DA_HEAD), lambda *_: (layer, 0, 0))] * 4 + \
           [pl.BlockSpec((None, 1, 2 * DA_HEAD), lambda *_: (layer, 0, 0))]


def _diff_attention(qda, kda, vda, lams, subln, layer, lam_init, nb, T):
    R = nb * SEQ
    qpb = SEQ // DA_QB
    meta_blk0 = R // N_META
    return pl.pallas_call(
        functools.partial(_da_kernel, lam_init=lam_init),
        grid=(nb, qpb),
        in_specs=[
            pl.BlockSpec((DA_QB, 512), lambda b, j: (b * qpb + j, 0)),
            pl.BlockSpec((SEQ, 512), lambda b, j: (b, 0)),
            pl.BlockSpec((SEQ, 512), lambda b, j: (b, 0)),
            pl.BlockSpec((N_META, 512), lambda b, j: (meta_blk0 + b, 0)),
            pl.BlockSpec((N_META, 512), lambda b, j: (meta_blk0 + b, 0)),
        ] + _lam_specs(layer),
        out_specs=pl.BlockSpec((DA_QB, 512), lambda b, j: (b * qpb + j, 0)),
        out_shape=jax.ShapeDtypeStruct((T, 512), BF16),
        compiler_params=_cparams(("parallel", "parallel")),
        name="diff_attn",
    )(qda, kda, vda, kda, vda, *lams, subln)


def _na_union_start(g):
    return min(max(NA_QR * g - NA_KH // 2, 0), ROWS - NA_KH, ROWS - NA_UR)


def _na_table_id(g):
    return 0 if g == 0 else (2 if g == NA_GROUPS - 1 else 1)


def _na_bias_tables(rpb):
    depth = rpb.shape[0]
    qc = np.arange(GRID_W)
    kc = np.arange(GRID_W)
    dc = np.clip(kc[None, :] - qc[:, None], -(NA_KW - 1), NA_KW - 1) + (NA_KW - 1)
    onehot = (dc[None] == np.arange(2 * NA_KW - 1)[:, None, None]).astype(np.float32)
    cs = np.clip(qc - NA_KW // 2, 0, GRID_W - NA_KW)
    col_ok = (kc[None, :] >= cs[:, None]) & (kc[None, :] < cs[:, None] + NA_KW)
    t = jnp.einsum("lhrc,cqk->lhrqk", rpb.astype(F32), jnp.asarray(onehot), precision=lax.Precision.HIGHEST)
    t = jnp.where(jnp.asarray(col_ok), t * LOG2E, -jnp.inf)
    neg = jnp.full((depth, NA_HEADS, GRID_W, GRID_W), -jnp.inf, F32)
    tabs = []
    for g in (0, 1, NA_GROUPS - 1):
        us = _na_union_start(g)
        q_rows = []
        for ri in range(NA_QR):
            r = NA_QR * g + ri
            rs = min(max(r - NA_KH // 2, 0), ROWS - NA_KH)
            blocks = [t[:, :, us + ki - r + NA_KH - 1] if rs <= us + ki < rs + NA_KH else neg
                      for ki in range(NA_UR)]
            q_rows.append(jnp.concatenate(blocks, axis=-1))
        tabs.append(jnp.concatenate(q_rows, axis=-2))
    return jnp.stack(tabs, axis=1)


def _na_kernel(q_ref, k_ref, v_ref, km_ref, vm_ref, bias_ref, o_ref):
    g = pl.program_id(0)
    us = jnp.minimum(jnp.clip(NA_QR * g - NA_KH // 2, 0, ROWS - NA_KH), ROWS - NA_UR)
    start = pl.multiple_of(us * GRID_W, GRID_W)
    for hp in range(NA_HEADS // 2):
        sl = slice(hp * 128, (hp + 1) * 128)
        qq = _split_halves(q_ref[:, sl])
        kw = k_ref[pl.ds(start, NA_UN), sl]
        vw = v_ref[pl.ds(start, NA_UN), sl]
        s = lax.dot_general(qq, kw, NT_DIMS, preferred_element_type=F32)
        s = s + jnp.concatenate([bias_ref[2 * hp], bias_ref[2 * hp + 1]], axis=0)
        sm = lax.dot_general(qq, km_ref[:, sl], NT_DIMS, preferred_element_type=F32)
        m = jnp.maximum(jnp.max(s, axis=-1, keepdims=True), jnp.max(sm, axis=-1, keepdims=True))
        p = jnp.exp2(s - m)
        pm = jnp.exp2(sm - m)
        inv = 1.0 / (jnp.sum(p, axis=-1, keepdims=True) + jnp.sum(pm, axis=-1, keepdims=True))
        o2 = (jnp.dot((p * inv).astype(BF16), vw, preferred_element_type=F32)
              + jnp.dot((pm * inv).astype(BF16), vm_ref[:, sl], preferred_element_type=F32))
        lane = lax.broadcasted_iota(jnp.int32, (NA_QN, 128), 1)
        o_ref[:, sl] = jnp.where(lane < 64, o2[:NA_QN], o2[NA_QN:]).astype(BF16)


def _na_attention(qna, kna, vna, bias, layer, nb, T):
    R = nb * SEQ
    meta_blk0 = R // N_META
    last = NA_GROUPS - 1

    def tid(g, b):
        return (layer, jnp.where(g == 0, 0, jnp.where(g == last, 2, 1)), 0, 0, 0)

    return pl.pallas_call(
        _na_kernel,
        grid=(NA_GROUPS, nb),
        in_specs=[
            pl.BlockSpec((NA_QN, NA_COLS), lambda g, b: (b * NA_GROUPS + g, 0)),
            pl.BlockSpec((SEQ, NA_COLS), lambda g, b: (b, 0)),
            pl.BlockSpec((SEQ, NA_COLS), lambda g, b: (b, 0)),
            pl.BlockSpec((N_META, NA_COLS), lambda g, b: (meta_blk0 + b, 0)),
            pl.BlockSpec((N_META, NA_COLS), lambda g, b: (meta_blk0 + b, 0)),
            pl.BlockSpec((None, None, NA_HEADS, NA_QN, NA_UN), tid),
        ],
        out_specs=pl.BlockSpec((NA_QN, NA_COLS), lambda g, b: (b * NA_GROUPS + g, 0)),
        out_shape=jax.ShapeDtypeStruct((T, NA_COLS), BF16),
        compiler_params=_cparams(("parallel", "parallel")),
        name="nbhd_attn",
    )(qna, kna, vna, kna, vna, bias)


def _meta_kernel(qd_ref, kr_ref, vr_ref, kdm_ref, vdm_ref, qn_ref, knm_ref, vnm_ref,
                 lq1_ref, lk1_ref, lq2_ref, lk2_ref, sg_ref, oda_in, ona_in, oda_ref, ona_ref,
                 *, lam_init, nb):
    del oda_in, ona_in
    i = pl.program_id(0)

    @pl.when(i < nb)
    def _():
        lam = _lambda(lq1_ref, lk1_ref, lq2_ref, lk2_ref, lam_init)
        _diff_attend(qd_ref[...], kr_ref, vr_ref, kdm_ref, vdm_ref, lam, sg_ref[...], lam_init, oda_ref)
        for hp in range(NA_HEADS // 2):
            sl = slice(hp * 128, (hp + 1) * 128)
            qq = _split_halves(qn_ref[:, sl])
            s = lax.dot_general(qq, knm_ref[:, sl], NT_DIMS, preferred_element_type=F32)
            p = jnp.exp2(s - jnp.max(s, axis=-1, keepdims=True))
            p = p * (1.0 / jnp.sum(p, axis=-1, keepdims=True))
            o2 = jnp.dot(p.astype(BF16), vnm_ref[:, sl], preferred_element_type=F32)
            lane = lax.broadcasted_iota(jnp.int32, (N_META, 128), 1)
            ona_ref[:, sl] = jnp.where(lane < 64, o2[:N_META], o2[N_META:]).astype(BF16)

    @pl.when(i >= nb)
    def _():
        oda_ref[...] = jnp.zeros_like(oda_ref)
        ona_ref[...] = jnp.zeros_like(ona_ref)


def _meta_queries(qda, kda, vda, qna, kna, vna, lams, subln, o_da, o_na, layer, lam_init, nb, T):
    R = nb * SEQ
    meta_blk0 = R // N_META
    n_steps = (T - R) // N_META

    def seq(i):
        return (jnp.minimum(i, nb - 1), 0)

    def meta_in(i):
        return (meta_blk0 + jnp.minimum(i, nb - 1), 0)

    def meta_out(i):
        return (meta_blk0 + i, 0)

    mspec = pl.BlockSpec((N_META, 512), meta_in)
    return pl.pallas_call(
        functools.partial(_meta_kernel, lam_init=lam_init, nb=nb),
        grid=(n_steps,),
        in_specs=[mspec, pl.BlockSpec((SEQ, 512), seq), pl.BlockSpec((SEQ, 512), seq), mspec, mspec,
                  mspec, mspec, mspec] + _lam_specs(layer) +
                 [pl.BlockSpec(memory_space=pl.ANY), pl.BlockSpec(memory_space=pl.ANY)],
        out_specs=[pl.BlockSpec((N_META, 512), meta_out), pl.BlockSpec((N_META, 512), meta_out)],
        out_shape=[jax.ShapeDtypeStruct((T, 512), BF16), jax.ShapeDtypeStruct((T, 512), BF16)],
        input_output_aliases={13: 0, 14: 1},
        compiler_params=_cparams(("arbitrary",)),
        name="meta_queries",
    )(qda, kda, vda, kda, vda, qna, kna, vna, *lams, subln, o_da, o_na)


def _merge_kernel(oda_ref, ona_ref, sga_ref, sgb_ref, h_ref, wa_ref, wb_ref, wo_ref, out_ref):
    ya = jnp.dot(oda_ref[...], wa_ref[...], preferred_element_type=F32)
    yb = jnp.dot(ona_ref[...], wb_ref[...], preferred_element_type=F32)
    merged = sga_ref[...].astype(F32) * ya + sgb_ref[...].astype(F32) * yb
    out_ref[...] = h_ref[...] + jnp.dot(merged.astype(BF16), wo_ref[...], preferred_element_type=F32)


def _merge(o_da, o_na, sga, sgb, h, wa, wb, wo, layer):
    T = h.shape[0]
    mb = 2 * TOK if T % (2 * TOK) == 0 else TOK

    def row(i):
        return (i, 0)

    def lyr(i):
        return (layer, 0, 0)

    return pl.pallas_call(
        _merge_kernel,
        grid=(T // mb,),
        in_specs=[
            pl.BlockSpec((mb, 512), row), pl.BlockSpec((mb, 512), row),
            pl.BlockSpec((mb, D_MODEL), row), pl.BlockSpec((mb, D_MODEL), row),
            pl.BlockSpec((mb, D_MODEL), row),
            pl.BlockSpec((None, 512, D_MODEL), lyr), pl.BlockSpec((None, 512, D_MODEL), lyr),
            pl.BlockSpec((None, D_MODEL, D_MODEL), lyr),
        ],
        out_specs=pl.BlockSpec((mb, D_MODEL), row),
        out_shape=jax.ShapeDtypeStruct((T, D_MODEL), F32),
        input_output_aliases={4: 0},
        compiler_params=_cparams(("parallel",)),
        name="merge",
    )(o_da, o_na, sga, sgb, h, wa, wb, wo)


def _gate_chunk(i0, tiles, hid_ref, wT_ref, r2_ref, p2_ref, cnti_ref, w1_ref):
    zero = jnp.zeros((), BF16)
    for t in tiles:
        ls = slice(t * LANES, (t + 1) * LANES)
        for kb in range(PEER_NKEYS // GATE_KEYS):
            ks = slice(kb * GATE_KEYS, (kb + 1) * GATE_KEYS)
            for ig in range(PEER_IPC // GATE_IG):
                acc = [None] * GATE_IG
                for h in range(PEER_HEADS):
                    r2t = r2_ref[h, ks, ls]
                    p2t = p2_ref[h, ks, ls]
                    cnt8 = cnti_ref[h, i0:i0 + PEER_IPC, ls]
                    w8 = w1_ref[h, i0:i0 + PEER_IPC, ls]
                    for q in range(GATE_IG):
                        ii = ig * GATE_IG + q
                        cb = jnp.broadcast_to(cnt8[ii:ii + 1], (16, LANES)).astype(BF16)
                        wb = jnp.broadcast_to(w8[ii:ii + 1], (16, LANES)).astype(BF16)
                        cb = jnp.tile(cb, (GATE_KEYS // 16, 1))
                        wb = jnp.tile(wb, (GATE_KEYS // 16, 1))
                        term = jnp.where(r2t < cb, p2t * wb, zero)
                        acc[q] = term if acc[q] is None else acc[q] + term
                for q in range(GATE_IG):
                    ii = ig * GATE_IG + q
                    rows = slice(ii * PEER_NKEYS + kb * GATE_KEYS, ii * PEER_NKEYS + (kb + 1) * GATE_KEYS)
                    hid = hid_ref[rows, ls]
                    act = 0.5 * hid * (1.0 + lax.erf(hid * math.sqrt(0.5)))
                    wT_ref[rows, ls] = acc[q] * act.astype(BF16)


def _peer_kernel(h_ref, g_ref, wqT_ref, keys_ref, u_ref, vTa_ref, vTb_ref, vTl_ref, out_ref,
                 cT_ref, qT_ref, s_ref, rk_ref, srt_ref, r2_ref, w1_ref, p2_ref, cnti_ref,
                 s1h_ref, s2h_ref, s1s_ref, s2s_ref, cnt_ref, cnth_ref, zinv_ref,
                 hida_ref, hidb_ref, wTa_ref, wTb_ref, acc_ref, cnts_ref, w1s_ref):
    j = pl.program_id(1)
    ntile = TOK // LANES

    def top_ranks(s):
        kiota = lax.broadcasted_iota(jnp.int32, (PEER_NKEYS, TOK), 0).astype(F32)
        riota = lax.broadcasted_iota(jnp.int32, (PEER_TOPK, TOK), 0)
        unranked = jnp.full((PEER_NKEYS, TOK), float(PEER_TOPK), F32)
        srt0 = jnp.zeros((PEER_TOPK, TOK), F32)

        def peel(exact):
            def body(a, srt):
                sv = s_ref[...]
                m = jnp.max(sv, axis=0, keepdims=True)
                hit = sv == m
                if exact:
                    first = jnp.min(jnp.where(hit, kiota, float(PEER_NKEYS)), axis=0, keepdims=True)
                    hit = kiota == first
                s_ref[...] = jnp.where(hit, -jnp.inf, sv)
                rk_ref[...] = jnp.where(hit, lax.convert_element_type(a, F32), rk_ref[...])
                return jnp.where(riota == a, m, srt)
            return body

        s_ref[...] = s
        rk_ref[...] = unranked
        srt_ref[...] = lax.fori_loop(0, PEER_TOPK, peel(False), srt0)
        ranked = jnp.sum(jnp.where(rk_ref[...] < float(PEER_TOPK), 1.0, 0.0), axis=0, keepdims=True)
        tied = jnp.max(ranked) > float(PEER_TOPK)

        @pl.when(tied)
        def _():
            s_ref[...] = s
            rk_ref[...] = unranked
            srt_ref[...] = lax.fori_loop(0, PEER_TOPK, peel(True), srt0)

        return rk_ref[...], srt_ref[...]

    @pl.when(j == 0)
    def _route():
        wTb_ref[...] = jnp.zeros_like(wTb_ref)
        acc_ref[...] = jnp.zeros_like(acc_ref)
        c = _rms(h_ref[...], g_ref[...])
        cT_ref[...] = c.T.astype(BF16)
        qT_ref[...] = jnp.dot(wqT_ref[...], cT_ref[...], preferred_element_type=F32).astype(BF16)

        def per_head(h, carry):
            rows = pl.ds(pl.multiple_of(h * PEER_NKEYS, PEER_NKEYS), PEER_NKEYS)
            q1 = qT_ref[pl.ds(pl.multiple_of(2 * h * PEER_NKEYS, PEER_NKEYS), PEER_NKEYS), :]
            q2 = qT_ref[pl.ds(pl.multiple_of((2 * h + 1) * PEER_NKEYS, PEER_NKEYS), PEER_NKEYS), :]
            s1 = jnp.dot(keys_ref[h, 0], q1, preferred_element_type=F32)
            rk, srt = top_ranks(s1)
            hida_ref[rows, :] = rk
            w1_ref[h] = jnp.exp(s1 - srt[0:1])
            s1h_ref[h] = srt
            s2 = jnp.dot(keys_ref[h, 1], q2, preferred_element_type=F32)
            rk, srt = top_ranks(s2)
            r2_ref[h] = rk.astype(BF16)
            p2_ref[h] = jnp.exp(s2 - srt[0:1]).astype(BF16)
            s2h_ref[h] = srt
            return carry

        lax.fori_loop(0, PEER_HEADS, per_head, 0)

        for h in range(PEER_HEADS):
            for a in range(PEER_TOPK):
                s1s_ref[a, h:h + 1, :] = s1h_ref[h, a:a + 1, :]
                s2s_ref[a, h:h + 1, :] = s2h_ref[h, a:a + 1, :]

        def joint(t, carry):
            ls = pl.ds(pl.multiple_of(t * LANES, LANES), LANES)
            s1 = [s1s_ref[a, :, ls] for a in range(PEER_TOPK)]
            s2 = [s2s_ref[b, :, ls] for b in range(PEER_TOPK)]
            cs = [s1[a] + s2[b] for (a, b) in PEER_CAND]
            ahead = [jnp.zeros((PEER_HEADS, LANES), F32) for _ in range(NCAND)]
            behind = [jnp.zeros((PEER_HEADS, LANES), F32) for _ in range(NCAND)]
            for c in range(NCAND):
                for c2 in range(c):
                    w = jnp.where(cs[c2] >= cs[c], 1.0, 0.0)
                    ahead[c] = ahead[c] + w
                    behind[c2] = behind[c2] + w
            e1 = [jnp.exp(s1[a] - s1[0]) for a in range(PEER_TOPK)]
            e2 = [jnp.exp(s2[b] - s2[0]) for b in range(PEER_TOPK)]
            z = jnp.zeros((PEER_HEADS, LANES), F32)
            cnt = [jnp.zeros((PEER_HEADS, LANES), F32) for _ in range(PEER_TOPK)]
            for c, (a, b) in enumerate(PEER_CAND):
                rank = ahead[c] + (float(NCAND - 1 - c) - behind[c])
                sel = jnp.where(rank < float(PEER_TOPK), 1.0, 0.0)
                cnt[a] = cnt[a] + sel
                z = z + sel * (e1[a] * e2[b])
            for a in range(PEER_TOPK):
                cnt_ref[a, :, ls] = cnt[a]
            zinv_ref[:, ls] = 1.0 / z
            return carry

        lax.fori_loop(0, ntile, joint, 0)

        for h in range(PEER_HEADS):
            for a in range(PEER_TOPK):
                cnth_ref[h, a:a + 1, :] = cnt_ref[a, h:h + 1, :]
            cnth_ref[h, PEER_TOPK:PEER_TOPK + 1, :] = zinv_ref[h:h + 1, :]

        def per_head_counts(h, carry):
            rows = pl.ds(pl.multiple_of(h * PEER_NKEYS, PEER_NKEYS), PEER_NKEYS)
            r1 = hida_ref[rows, :]
            cn = cnth_ref[h, 0:PEER_TOPK, :]
            ci = jnp.zeros((PEER_NKEYS, TOK), F32)
            for a in range(PEER_TOPK):
                ci = jnp.where(r1 == float(a), cn[a:a + 1], ci)
            cnti_ref[h] = ci
            w1_ref[h] = w1_ref[h] * cnth_ref[h, PEER_TOPK:PEER_TOPK + 1, :]
            return carry

        lax.fori_loop(0, PEER_HEADS, per_head_counts, 0)

    half = TOK // 2
    halves = (slice(0, half), slice(half, TOK))
    half_tiles = (tuple(range(0, ntile // 2)), tuple(range(ntile // 2, ntile)))
    step_rows = pl.ds(pl.multiple_of(j * (2 * PEER_IPC), 2 * PEER_IPC), 2 * PEER_IPC)
    for h in range(PEER_HEADS):
        cnts_ref[h] = cnti_ref[h, step_rows, :]
        w1s_ref[h] = w1_ref[h, step_rows, :]
    ia, ib = 0, PEER_IPC

    def mm_hid(hid_ref, urows, hf):
        hid_ref[:, halves[hf]] = jnp.dot(u_ref[urows, :], cT_ref[:, halves[hf]], preferred_element_type=F32)

    def mm_out(vT_ref, wT_ref, hf):
        acc_ref[:, halves[hf]] += jnp.dot(vT_ref[...], wT_ref[:, halves[hf]], preferred_element_type=F32)

    def gate(i0, hid_ref, wT_ref, hf):
        _gate_chunk(i0, half_tiles[hf], hid_ref, wT_ref, r2_ref, p2_ref, cnts_ref, w1s_ref)

    rows_a, rows_b = slice(0, PEER_EC), slice(PEER_EC, 2 * PEER_EC)
    mm_hid(hida_ref, rows_a, 0)
    mm_hid(hida_ref, rows_a, 1)
    mm_out(vTb_ref, wTb_ref, 0)
    gate(ia, hida_ref, wTa_ref, 0)
    mm_out(vTb_ref, wTb_ref, 1)
    mm_hid(hidb_ref, rows_b, 0)
    gate(ia, hida_ref, wTa_ref, 1)
    mm_out(vTa_ref, wTa_ref, 0)
    mm_hid(hidb_ref, rows_b, 1)
    gate(ib, hidb_ref, wTb_ref, 0)
    mm_out(vTa_ref, wTa_ref, 1)
    gate(ib, hidb_ref, wTb_ref, 1)

    @pl.when(j == PEER_NSTEP - 1)
    def _():
        last = jnp.dot(vTl_ref[...], wTb_ref[...], preferred_element_type=F32)
        out_ref[...] = h_ref[...] + (acc_ref[...] + last).T


def _peer(h, gains, wqT, keys, u, vT, layer):
    T = h.shape[0]
    hk = (PEER_HEADS, PEER_NKEYS, TOK)
    once = pl.Buffered(1)
    return pl.pallas_call(
        _peer_kernel,
        grid=(T // TOK, PEER_NSTEP),
        in_specs=[
            pl.BlockSpec((TOK, D_MODEL), lambda t, j: (t, 0)),
            pl.BlockSpec((None, 1, D_MODEL), lambda t, j: (layer, 0, 0)),
            pl.BlockSpec((None, PEER_HEADS * 256, D_MODEL), lambda t, j: (layer, 0, 0), pipeline_mode=once),
            pl.BlockSpec((None, PEER_HEADS, 2, PEER_NKEYS, 128), lambda t, j: (layer, 0, 0, 0, 0),
                         pipeline_mode=once),
            pl.BlockSpec((None, 2 * PEER_EC, D_MODEL), lambda t, j: (layer, j, 0)),
            pl.BlockSpec((None, None, D_MODEL, PEER_EC), lambda t, j: (layer, 2 * j, 0, 0)),
            pl.BlockSpec((None, None, D_MODEL, PEER_EC), lambda t, j: (layer, jnp.maximum(2 * j - 1, 0), 0, 0)),
            pl.BlockSpec((None, None, D_MODEL, PEER_EC), lambda t, j: (layer, PEER_NCH - 1, 0, 0),
                         pipeline_mode=once),
        ],
        out_specs=pl.BlockSpec((TOK, D_MODEL), lambda t, j: (t, 0)),
        out_shape=jax.ShapeDtypeStruct((T, D_MODEL), F32),
        scratch_shapes=[
            pltpu.VMEM((D_MODEL, TOK), BF16),
            pltpu.VMEM((PEER_HEADS * 256, TOK), BF16),
            pltpu.VMEM((PEER_NKEYS, TOK), F32),
            pltpu.VMEM((PEER_NKEYS, TOK), F32),
            pltpu.VMEM((PEER_TOPK, TOK), F32),
            pltpu.VMEM(hk, BF16),
            pltpu.VMEM(hk, F32),
            pltpu.VMEM(hk, BF16),
            pltpu.VMEM(hk, F32),
            pltpu.VMEM((PEER_HEADS, PEER_TOPK, TOK), F32),
            pltpu.VMEM((PEER_HEADS, PEER_TOPK, TOK), F32),
            pltpu.VMEM((PEER_TOPK, PEER_HEADS, TOK), F32),
            pltpu.VMEM((PEER_TOPK, PEER_HEADS, TOK), F32),
            pltpu.VMEM((PEER_TOPK, PEER_HEADS, TOK), F32),
            pltpu.VMEM((PEER_HEADS, PEER_TOPK + 8, TOK), F32),
            pltpu.VMEM((PEER_HEADS, TOK), F32),
            pltpu.VMEM((PEER_EC, TOK), F32),
            pltpu.VMEM((PEER_EC, TOK), F32),
            pltpu.VMEM((PEER_EC, TOK), BF16),
            pltpu.VMEM((PEER_EC, TOK), BF16),
            pltpu.VMEM((D_MODEL, TOK), F32),
            pltpu.VMEM((PEER_HEADS, 2 * PEER_IPC, TOK), F32),
            pltpu.VMEM((PEER_HEADS, 2 * PEER_IPC, TOK), F32),
        ],
        input_output_aliases={0: 0},
        compiler_params=_cparams(("parallel", "arbitrary")),
        name="peer",
    )(h, gains, wqT, keys, u, vT, vT, vT)


def _final_kernel(h_ref, g_ref, o_ref):
    o_ref[...] = _rms(h_ref[...], g_ref[...])


def _final_norm(h, gain, blk0, nblk):
    return pl.pallas_call(
        _final_kernel,
        grid=(nblk,),
        in_specs=[pl.BlockSpec((TOK, D_MODEL), lambda i: (blk0 + i, 0)),
                  pl.BlockSpec((1, D_MODEL), lambda i: (0, 0))],
        out_specs=pl.BlockSpec((TOK, D_MODEL), lambda i: (i, 0)),
        out_shape=jax.ShapeDtypeStruct((nblk * TOK, D_MODEL), F32),
        compiler_params=_cparams(("parallel",)),
        name="final_norm",
    )(h, gain)


def _rope_tables():
    half = DA_HEAD // 2
    inv = 1.0 / (ROPE_THETA ** (jnp.arange(half, dtype=F32) * 2.0 / DA_HEAD))
    pos = jnp.concatenate([jnp.arange(N_META, N_META + SEQ, dtype=F32),
                           jnp.tile(jnp.arange(N_META, dtype=F32), TOK // N_META)])
    ang = pos[:, None] * inv[None, :]
    cos, sin = jnp.cos(ang), jnp.sin(ang)
    cos_t = jnp.tile(cos, (1, 512 // half))
    sin_t = jnp.tile(jnp.concatenate([-sin, sin], axis=1), (1, 512 // DA_HEAD))
    return cos_t, sin_t


def kernel(x_prompt, x_sample, meta_tokens, norm_mix, w_in, lambda_q1, lambda_k1, lambda_q2, lambda_k2,
           subln_gain, na_rpb, w_branch_a, w_branch_b, w_out, norm_ffn, peer_wq, peer_keys, peer_u,
           peer_v, norm_final):
    depth = w_in.shape[0]
    nb1, nb2 = x_prompt.shape[0], x_sample.shape[0]
    nb = nb1 + nb2
    R = nb * SEQ
    T = R + _round_up(nb * N_META, TOK)

    h = jnp.concatenate([
        x_prompt.reshape(nb1 * SEQ, D_MODEL), x_sample.reshape(nb2 * SEQ, D_MODEL),
        jnp.broadcast_to(meta_tokens[None], (nb, N_META, D_MODEL)).reshape(nb * N_META, D_MODEL),
        jnp.zeros((T - R - nb * N_META, D_MODEL), F32)], axis=0)

    swap = np.array([(c // DA_HEAD) * DA_HEAD + (c % DA_HEAD + DA_HEAD // 2) % DA_HEAD for c in range(512)])
    wq, wk = w_in[:, :, :512], w_in[:, :, 512:1024]
    w_ext = jnp.concatenate([wq, wq[:, :, swap], wk, wk[:, :, swap], w_in[:, :, 1024:]], axis=2).astype(BF16)
    cos_t, sin_t = _rope_tables()
    bias = _na_bias_tables(na_rpb)
    wa, wb, wo = w_branch_a.astype(BF16), w_branch_b.astype(BF16), w_out.astype(BF16)
    wqT = jnp.transpose(peer_wq, (0, 2, 1)).astype(BF16)
    keys = peer_keys.astype(BF16)
    u = peer_u.astype(BF16)
    vT = jnp.transpose(peer_v.reshape(depth, PEER_NCH, PEER_EC, D_MODEL), (0, 1, 3, 2)).astype(BF16)
    lams = [x.reshape(depth, 1, DA_HEAD) for x in (lambda_q1, lambda_k1, lambda_q2, lambda_k2)]
    subln = subln_gain.reshape(depth, 1, 2 * DA_HEAD)
    g_mix = norm_mix.reshape(depth, 1, D_MODEL)
    g_ffn = norm_ffn.reshape(depth, 1, D_MODEL)

    for l in range(depth):
        lam_init = 0.8 - 0.6 * math.exp(-0.3 * l)
        qda, kda, vda, qna, kna, vna, sga, sgb = _in_proj(h, g_mix, w_ext, cos_t, sin_t, l, R // TOK)
        o_da = _diff_attention(qda, kda, vda, lams, subln, l, lam_init, nb, T)
        o_na = _na_attention(qna, kna, vna, bias, l, nb, T)
        o_da, o_na = _meta_queries(qda, kda, vda, qna, kna, vna, lams, subln, o_da, o_na, l, lam_init, nb, T)
        h = _merge(o_da, o_na, sga, sgb, h, wa, wb, wo, l)
        h = _peer(h, g_ffn, wqT, keys, u, vT, l)

    gf = norm_final.reshape(1, D_MODEL)
    y1 = _final_norm(h, gf, 0, nb1 * SEQ // TOK).reshape(nb1, SEQ, D_MODEL)
    y2 = _final_norm(h, gf, nb1 * SEQ // TOK, nb2 * SEQ // TOK).reshape(nb2, SEQ, D_MODEL)
    return (y1, y2)
```
